```python
import math
import jax, jax.numpy as jnp
from jax import lax
import numpy as np

D_MODEL = 1024
BATCH = 2
SEQ = 16384
DEPTH = 2

GRID_W = 64
CTX_LEN = 256
HEAD_DIM = 64
ATTN_WIDTH = D_MODEL // 2
N_HEADS = ATTN_WIDTH // HEAD_DIM
N_KV_HEADS = N_HEADS // 4
KV_WIDTH = N_KV_HEADS * HEAD_DIM
WINDOW = 128
ATTN_BLOCK = 128
ROPE_BASE = 10000.0
S5_WIDTH = D_MODEL // 4
S5_GROUP = 16
S5_GROUPS = S5_WIDTH // S5_GROUP
S5_STATE = 64
S5_DIRS = 2
SGU_WIDTH = D_MODEL // 4
SGU_GROUPS = 4
SGU_CHUNK = 128
MIX_WIDTH = ATTN_WIDTH + S5_WIDTH + SGU_WIDTH
IN_WIDTH = ATTN_WIDTH + 2 * KV_WIDTH + S5_WIDTH + 2 * SGU_WIDTH
IN_SPLITS = (ATTN_WIDTH, ATTN_WIDTH + KV_WIDTH, ATTN_WIDTH + 2 * KV_WIDTH, ATTN_WIDTH + 2 * KV_WIDTH + S5_WIDTH)
OUT_SPLITS = (ATTN_WIDTH, ATTN_WIDTH + S5_WIDTH)
N_EXPERTS = 16
EXPERT_FF = D_MODEL
EC_CAPACITY_FACTOR = 2
N_MOD = 6
EPS = 1e-6

kernel_name = 'hybrid_s5_sgu_swa_ecmoe_diffusion'

F32 = jnp.float32


def rmsnorm(x, g):
    x32 = x.astype(F32)
    y = x32 * lax.rsqrt(jnp.mean(x32 * x32, axis=-1, keepdims=True) + EPS)
    return (y * g.astype(F32)).astype(x.dtype)


def modulate(x, g, shift, scale):
    return rmsnorm(x, g) * (1 + scale) + shift


def axial_rope_tables(n):
    rows = n // GRID_W
    row = jnp.repeat(jnp.arange(rows), GRID_W).astype(F32)
    col = jnp.tile(jnp.arange(GRID_W), rows).astype(F32)
    axis_dim = HEAD_DIM // 2
    inv = ROPE_BASE ** (-jnp.arange(0, axis_dim, 2, dtype=F32) / axis_dim)
    ang_r = row[:, None] * inv
    ang_c = col[:, None] * inv
    return (jnp.cos(ang_r), jnp.sin(ang_r), jnp.cos(ang_c), jnp.sin(ang_c))


def _rotate(x, cos, sin):
    x1, x2 = jnp.split(x, 2, axis=-1)
    cos = cos[:, None, :]
    sin = sin[:, None, :]
    return jnp.concatenate([x1 * cos - x2 * sin, x2 * cos + x1 * sin], axis=-1)


def apply_axial_rope(x, tabs):
    cr, sr, cc, sc = tabs
    xr, xc = jnp.split(x, 2, axis=-1)
    return jnp.concatenate([_rotate(xr, cr, sr), _rotate(xc, cc, sc)], axis=-1).astype(x.dtype)


def window_attention(q, k, v, k_ctx, v_ctx, sink):
    b, n = q.shape[0], q.shape[1]
    nb = n // ATTN_BLOCK
    grp = N_HEADS // N_KV_HEADS
    scale = HEAD_DIM ** -0.5
    qb = q.reshape(b, nb, ATTN_BLOCK, N_KV_HEADS, grp, HEAD_DIM)

    def band(t):
        tp = jnp.pad(t, ((0, 0), (ATTN_BLOCK, ATTN_BLOCK), (0, 0), (0, 0)))
        tp = tp.reshape(b, nb + 2, ATTN_BLOCK, N_KV_HEADS, HEAD_DIM)
        return jnp.concatenate([tp[:, :-2], tp[:, 1:-1], tp[:, 2:]], axis=2)

    kb, vb = band(k), band(v)
    qi = jnp.arange(nb)[:, None, None] * ATTN_BLOCK + jnp.arange(ATTN_BLOCK)[None, :, None]
    kj = (jnp.arange(nb)[:, None, None] - 1) * ATTN_BLOCK + jnp.arange(3 * ATTN_BLOCK)[None, None, :]
    valid = (jnp.abs(kj - qi) <= WINDOW) & (kj >= 0) & (kj < n)

    s_win = jnp.einsum('bnqkgd,bnskd->bnkgqs', qb, kb).astype(F32) * scale
    s_win = jnp.where(valid[None, :, None, None], s_win, -jnp.inf)
    s_ctx = jnp.einsum('bnqkgd,bckd->bnkgqc', qb, k_ctx).astype(F32) * scale
    s_sink = sink.astype(F32).reshape(1, 1, N_KV_HEADS, grp, 1, 1)
    m = jnp.maximum(jnp.maximum(s_win.max(-1, keepdims=True), s_ctx.max(-1, keepdims=True)), s_sink)
    e_win = jnp.exp(s_win - m)
    e_ctx = jnp.exp(s_ctx - m)
    denom = e_win.sum(-1, keepdims=True) + e_ctx.sum(-1, keepdims=True) + jnp.exp(s_sink - m)
    p_win = (e_win / denom).astype(v.dtype)
    p_ctx = (e_ctx / denom).astype(v.dtype)
    o = jnp.einsum('bnkgqs,bnskd->bnqkgd', p_win, vb) + jnp.einsum('bnkgqc,bckd->bnqkgd', p_ctx, v_ctx)
    return o.reshape(b, n, N_HEADS * HEAD_DIM)


def context_attention(q, k, v, sink):
    b, lc = q.shape[0], q.shape[1]
    grp = N_HEADS // N_KV_HEADS
    qg = q.reshape(b, lc, N_KV_HEADS, grp, HEAD_DIM)
    s = jnp.einsum('bqkgd,bskd->bkgqs', qg, k).astype(F32) * HEAD_DIM ** -0.5
    s_sink = jnp.broadcast_to(sink.astype(F32).reshape(1, N_KV_HEADS, grp, 1, 1), s.shape[:-1] + (1,))
    p = jax.nn.softmax(jnp.concatenate([s, s_sink], axis=-1), axis=-1)[..., :-1].astype(v.dtype)
    o = jnp.einsum('bkgqs,bskd->bqkgd', p, v)
    return o.reshape(b, lc, N_HEADS * HEAD_DIM)


def s5_discretize(lam_re, lam_im, log_dt, b_re, b_im):
    dt = jnp.exp(log_dt.astype(F32))[:, None]
    lr, li = lam_re.astype(F32), lam_im.astype(F32)
    mag = jnp.exp(lr * dt)
    ang = li * dt
    ab_re, ab_im = mag * jnp.cos(ang), mag * jnp.sin(ang)
    nr, ni = ab_re - 1.0, ab_im
    den = lr * lr + li * li
    coef_re = (nr * lr + ni * li) / den
    coef_im = (ni * lr - nr * li) / den
    br, bi = b_re.astype(F32), b_im.astype(F32)
    bb_re = coef_re[..., None] * br - coef_im[..., None] * bi
    bb_im = coef_re[..., None] * bi + coef_im[..., None] * br
    return ab_re, ab_im, bb_re, bb_im


def _linear_recurrence_combine(e1, e2):
    a1r, a1i, b1r, b1i = e1
    a2r, a2i, b2r, b2i = e2
    return (a2r * a1r - a2i * a1i,
            a2r * a1i + a2i * a1r,
            a2r * b1r - a2i * b1i + b2r,
            a2r * b1i + a2i * b1r + b2i)


def s5_scan(u, disc, h0, reverse):
    ab_re, ab_im, bb_re, bb_im = disc
    bu_re = jnp.einsum('gpc,bngc->bngp', bb_re, u)
    bu_im = jnp.einsum('gpc,bngc->bngp', bb_im, u)
    if h0 is not None:
        h0_re, h0_im = h0
        pos = -1 if reverse else 0
        bu_re = bu_re.at[:, pos].add(ab_re * h0_re - ab_im * h0_im)
        bu_im = bu_im.at[:, pos].add(ab_re * h0_im + ab_im * h0_re)
    a_re = jnp.broadcast_to(ab_re, bu_re.shape)
    a_im = jnp.broadcast_to(ab_im, bu_im.shape)
    _, _, h_re, h_im = lax.associative_scan(_linear_recurrence_combine, (a_re, a_im, bu_re, bu_im),
                                            reverse=reverse, axis=1)
    return h_re, h_im


def s5_readout(c_re, c_im, h_re, h_im):
    y = jnp.einsum('gcp,bngp->bngc', c_re.astype(F32), h_re) - jnp.einsum('gcp,bngp->bngc', c_im.astype(F32), h_im)
    return y.reshape(y.shape[0], y.shape[1], S5_WIDTH)


def s5_mixer(u_lat, u_ctx, lam_re, lam_im, log_dt, b_re, b_im, c_re, c_im, d_skip, glu_w, glu_b, ctx_out):
    def groups(u):
        return u.astype(F32).reshape(u.shape[0], u.shape[1], S5_GROUPS, S5_GROUP)

    ul, uc = groups(u_lat), groups(u_ctx)
    d32 = d_skip.astype(F32)
    y_lat = d32 * u_lat.astype(F32)
    y_ctx = d32 * u_ctx.astype(F32)
    for direction in range(S5_DIRS):
        reverse = direction == 1
        disc = s5_discretize(lam_re[direction], lam_im[direction], log_dt[direction],
                             b_re[direction], b_im[direction])
        hc_re, hc_im = s5_scan(uc, disc, None, reverse)
        end = 0 if reverse else -1
        hl_re, hl_im = s5_scan(ul, disc, (hc_re[:, end], hc_im[:, end]), reverse)
        y_lat = y_lat + s5_readout(c_re[direction], c_im[direction], hl_re, hl_im)
        if ctx_out:
            y_ctx = y_ctx + s5_readout(c_re[direction], c_im[direction], hc_re, hc_im)

    def glu(y, dtype):
        g = jax.nn.gelu(y)
        return (g * jax.nn.sigmoid(g @ glu_w.astype(F32) + glu_b.astype(F32))).astype(dtype)

    return glu(y_lat, u_lat.dtype), (glu(y_ctx, u_ctx.dtype) if ctx_out else None)


def sgu_mixer(z, norm_g, w_s, b_s):
    u, v = jnp.split(jax.nn.gelu(z), 2, axis=-1)
    v = rmsnorm(v, norm_g)
    b, n = v.shape[0], v.shape[1]
    vc = v.reshape(b, n // SGU_CHUNK, SGU_CHUNK, SGU_GROUPS, SGU_WIDTH // SGU_GROUPS)
    s = jnp.einsum('gts,bnsgc->bntgc', w_s, vc) + b_s.T[None, None, :, :, None]
    return u * s.reshape(b, n, SGU_WIDTH)


def mixer_out_norm(y, g):
    parts = jnp.split(y, OUT_SPLITS, axis=-1)
    gains = jnp.split(g, OUT_SPLITS)
    return jnp.concatenate([rmsnorm(p, gg) for p, gg in zip(parts, gains)], axis=-1)


def expert_choice_moe(h, router_w, w_gate, w_up, w_down):
    b, n = h.shape[0], h.shape[1]
    cap = EC_CAPACITY_FACTOR * n // N_EXPERTS
    probs = jax.nn.softmax(jnp.einsum('bnd,de->ben', h, router_w).astype(F32), axis=1)
    gate, idx = lax.top_k(probs, cap)
    bidx = jnp.arange(b)[:, None, None]
    xin = h[bidx, idx]
    hid = jax.nn.silu(jnp.einsum('becd,edf->becf', xin, w_gate)) * jnp.einsum('becd,edf->becf', xin, w_up)
    y = jnp.einsum('becf,efd->becd', hid, w_down) * gate[..., None].astype(h.dtype)
    return jnp.zeros_like(h).at[bidx, idx].add(y)


def setup_inputs(seed: int = 0) -> dict:
    key = jax.random.key(seed)
    ks = iter(jax.random.split(key, 40))

    def nrm(shape, scale):
        return jax.random.normal(next(ks), shape, F32) * scale

    def gain(shape):
        return 1.0 + nrm(shape, 0.02)

    lam_im_base = jnp.pi * jnp.arange(S5_STATE, dtype=F32)
    return {
        'x': nrm((BATCH, SEQ, D_MODEL), 1.0),
        'c': nrm((BATCH, D_MODEL), 1.0),
        'ctx': nrm((BATCH, CTX_LEN, D_MODEL), 1.0),
        'c_ctx': nrm((D_MODEL,), 1.0),
        'ada_w': nrm((DEPTH, D_MODEL, N_MOD * D_MODEL), 0.5 * D_MODEL ** -0.5),
        'ada_b': nrm((DEPTH, N_MOD * D_MODEL), 0.02),
        'norm_mix_g': gain((DEPTH, D_MODEL)),
        'norm_ffn_g': gain((DEPTH, D_MODEL)),
        'w_in': nrm((DEPTH, D_MODEL, IN_WIDTH), D_MODEL ** -0.5),
        'q_norm_g': gain((DEPTH, HEAD_DIM)),
        'k_norm_g': gain((DEPTH, HEAD_DIM)),
        'attn_sink': nrm((DEPTH, N_HEADS), 0.5),
        's5_lambda_re': -0.5 + nrm((DEPTH, S5_DIRS, S5_GROUPS, S5_STATE), 0.01),
        's5_lambda_im': lam_im_base + nrm((DEPTH, S5_DIRS, S5_GROUPS, S5_STATE), 0.01),
        's5_log_dt': jax.random.uniform(next(ks), (DEPTH, S5_DIRS, S5_GROUPS), F32,
                                        minval=math.log(1e-3), maxval=math.log(1e-1)),
        's5_b_re': nrm((DEPTH, S5_DIRS, S5_GROUPS, S5_STATE, S5_GROUP), (2 * S5_GROUP) ** -0.5),
        's5_b_im': nrm((DEPTH, S5_DIRS, S5_GROUPS, S5_STATE, S5_GROUP), (2 * S5_GROUP) ** -0.5),
        's5_c_re': nrm((DEPTH, S5_DIRS, S5_GROUPS, S5_GROUP, S5_STATE), (2 * S5_STATE) ** -0.5),
        's5_c_im': nrm((DEPTH, S5_DIRS, S5_GROUPS, S5_GROUP, S5_STATE), (2 * S5_STATE) ** -0.5),
        's5_d': nrm((DEPTH, S5_WIDTH), 1.0),
        's5_glu_w': nrm((DEPTH, S5_WIDTH, S5_WIDTH), S5_WIDTH ** -0.5),
        's5_glu_b': nrm((DEPTH, S5_WIDTH), 0.02),
        'sgu_norm_g': gain((DEPTH, SGU_WIDTH)),
        'sgu_w': nrm((DEPTH, SGU_GROUPS, SGU_CHUNK, SGU_CHUNK), SGU_CHUNK ** -0.5),
        'sgu_b': 1.0 + nrm((DEPTH, SGU_GROUPS, SGU_CHUNK), 0.1),
        'out_norm_g': gain((DEPTH, MIX_WIDTH)),
        'w_out': nrm((DEPTH, MIX_WIDTH, D_MODEL), MIX_WIDTH ** -0.5),
        'router_w': nrm((DEPTH, D_MODEL, N_EXPERTS), D_MODEL ** -0.5),
        'exp_w_gate': nrm((DEPTH, N_EXPERTS, D_MODEL, EXPERT_FF), D_MODEL ** -0.5),
        'exp_w_up': nrm((DEPTH, N_EXPERTS, D_MODEL, EXPERT_FF), D_MODEL ** -0.5),
        'exp_w_down': nrm((DEPTH, N_EXPERTS, EXPERT_FF, D_MODEL), EXPERT_FF ** -0.5),
    }


def reference(x, c, ctx, c_ctx, ada_w, ada_b, norm_mix_g, norm_ffn_g, w_in, q_norm_g, k_norm_g, attn_sink,
              s5_lambda_re, s5_lambda_im, s5_log_dt, s5_b_re, s5_b_im, s5_c_re, s5_c_im, s5_d, s5_glu_w,
              s5_glu_b, sgu_norm_g, sgu_w, sgu_b, out_norm_g, w_out, router_w, exp_w_gate, exp_w_up,
              exp_w_down):
    b, n = x.shape[0], x.shape[1]
    rope = axial_rope_tables(n)
    h_lat, h_ctx = x, ctx
    silu_c = jax.nn.silu(c)
    silu_cc = jax.nn.silu(c_ctx)

    def heads(t, nh):
        return t.reshape(t.shape[0], t.shape[1], nh, HEAD_DIM)

    for l in range(DEPTH):
        ctx_out = l < DEPTH - 1
        mod_lat = (silu_c @ ada_w[l] + ada_b[l]).reshape(b, N_MOD, D_MODEL)[:, :, None, :]
        mod_ctx = (silu_cc @ ada_w[l] + ada_b[l]).reshape(N_MOD, D_MODEL)[None, :, None, :]

        a_lat = modulate(h_lat, norm_mix_g[l], mod_lat[:, 0], mod_lat[:, 1])
        a_ctx = modulate(h_ctx, norm_mix_g[l], mod_ctx[:, 0], mod_ctx[:, 1])
        q_l, k_l, v_l, s_l, g_l = jnp.split(a_lat @ w_in[l], IN_SPLITS, axis=-1)
        q_c, k_c, v_c, s_c, g_c = jnp.split(a_ctx @ w_in[l], IN_SPLITS, axis=-1)

        q_l = apply_axial_rope(rmsnorm(heads(q_l, N_HEADS), q_norm_g[l]), rope)
        k_l = apply_axial_rope(rmsnorm(heads(k_l, N_KV_HEADS), k_norm_g[l]), rope)
        v_l = heads(v_l, N_KV_HEADS)
        k_c = rmsnorm(heads(k_c, N_KV_HEADS), k_norm_g[l])
        v_c = heads(v_c, N_KV_HEADS)
        att_lat = window_attention(q_l, k_l, v_l, k_c, v_c, attn_sink[l])

        s5_lat, s5_ctx = s5_mixer(s_l, s_c, s5_lambda_re[l], s5_lambda_im[l], s5_log_dt[l], s5_b_re[l],
                                  s5_b_im[l], s5_c_re[l], s5_c_im[l], s5_d[l], s5_glu_w[l], s5_glu_b[l],
                                  ctx_out)

        sgu_lat = sgu_mixer(g_l, sgu_norm_g[l], sgu_w[l], sgu_b[l])

        mix_lat = mixer_out_norm(jnp.concatenate([att_lat, s5_lat, sgu_lat], axis=-1), out_norm_g[l])
        h_lat = h_lat + mod_lat[:, 2] * (mix_lat @ w_out[l])
        if ctx_out:
            q_c = rmsnorm(heads(q_c, N_HEADS), q_norm_g[l])
            att_ctx = context_attention(q_c, k_c, v_c, attn_sink[l])
            sgu_ctx = sgu_mixer(g_c, sgu_norm_g[l], sgu_w[l], sgu_b[l])
            mix_ctx = mixer_out_norm(jnp.concatenate([att_ctx, s5_ctx, sgu_ctx], axis=-1), out_norm_g[l])
            h_ctx = h_ctx + mod_ctx[:, 2] * (mix_ctx @ w_out[l])

        f_lat = modulate(h_lat, norm_ffn_g[l], mod_lat[:, 3], mod_lat[:, 4])
        h_lat = h_lat + mod_lat[:, 5] * expert_choice_moe(f_lat, router_w[l], exp_w_gate[l], exp_w_up[l],
                                                          exp_w_down[l])
        if ctx_out:
            f_ctx = modulate(h_ctx, norm_ffn_g[l], mod_ctx[:, 3], mod_ctx[:, 4])
            h_ctx = h_ctx + mod_ctx[:, 5] * expert_choice_moe(f_ctx, router_w[l], exp_w_gate[l],
                                                              exp_w_up[l], exp_w_down[l])
    return h_lat
```

```python
import functools

import numpy as np
import jax
import jax.numpy as jnp
from jax import lax
from jax.experimental import pallas as pl
from jax.experimental.pallas import tpu as pltpu

F32 = jnp.float32
BF16 = jnp.bfloat16

HEAD_DIM = 64
N_HEADS = 8
N_KV_HEADS = 2
KV_GROUP = N_HEADS // N_KV_HEADS
ATTN_WIDTH = N_HEADS * HEAD_DIM
WINDOW = 128
GRID_W = 64
ROPE_BASE = 10000.0
S5_WIDTH = 256
S5_GROUP = 16
S5_GROUPS = S5_WIDTH // S5_GROUP
S5_STATE = 64
S5_NSTATE = S5_GROUPS * S5_STATE
SGU_WIDTH = 256
SGU_GROUPS = 4
SGU_CHUNK = 128
N_EXPERTS = 16
EC_CAPACITY_FACTOR = 2
N_MOD = 6
EPS = 1e-6

LANES = 128
SUBLANES = 8
VMEM_LIMIT_BYTES = 56 * 1024 * 1024

ROW_TILE = 512
S5_SEGMENTS = SUBLANES
ROUTE_BLOCK = 128


def _cparams(n_axes):
    return pltpu.CompilerParams(dimension_semantics=("arbitrary",) * n_axes,
                                vmem_limit_bytes=VMEM_LIMIT_BYTES)


def _dot(a, b):
    return jnp.dot(a, b, preferred_element_type=F32)


def _dot_nt(a, b):
    return lax.dot_general(a, b, (((1,), (1,)), ((), ())), preferred_element_type=F32)


def _dot_tn(a, b):
    return lax.dot_general(a, b, (((0,), (0,)), ((), ())), preferred_element_type=F32)


def _split(x):
    hi = x.astype(BF16)
    lo = (x - hi.astype(F32)).astype(BF16)
    return hi, lo


def _sigmoid(x):
    return 1.0 / (1.0 + jnp.exp(-x))


def _gelu_tanh(x):
    c = np.float32(np.sqrt(2.0 / np.pi))
    return x * (0.5 * (1.0 + jnp.tanh(c * (x + np.float32(0.044715) * (x * x * x)))))


def _rms(x, g):
    return x * lax.rsqrt(jnp.mean(x * x, axis=-1, keepdims=True) + EPS) * g


def _ada_kernel(c_ref, w_ref, b_ref, o_ref):
    c = c_ref[...]
    sc = c * _sigmoid(c)
    ch, cl = _split(sc)
    wh, wl = _split(w_ref[0])
    o_ref[0] = _dot(ch, wh) + _dot(ch, wl) + _dot(cl, wh) + b_ref[0]


def _ada_call(crows, ada_w, ada_b):
    depth, d, nmod = ada_w.shape
    rows = crows.shape[0]
    tn = 1536
    assert nmod % tn == 0
    return pl.pallas_call(
        _ada_kernel,
        out_shape=jax.ShapeDtypeStruct((depth, rows, nmod), F32),
        grid=(depth, nmod // tn),
        in_specs=[pl.BlockSpec((rows, d), lambda l, j: (0, 0)),
                  pl.BlockSpec((1, d, tn), lambda l, j: (l, 0, j)),
                  pl.BlockSpec((1, 1, tn), lambda l, j: (l, 0, j))],
        out_specs=pl.BlockSpec((1, rows, tn), lambda l, j: (l, 0, j)),
        compiler_params=_cparams(2),
        name="ada_mod",
    )(crows, ada_w, ada_b.reshape(depth, 1, nmod))


IN_COLS = ATTN_WIDTH + 2 * (2 * N_KV_HEADS * HEAD_DIM) + S5_WIDTH + 2 * SGU_WIDTH
_QO, _KO, _VO, _SO, _GO = 0, 512, 768, 1024, 1280


def _head_sumsq(x, ones_blk):
    hi, lo = _split(x * x)
    return _dot(hi, ones_blk) + _dot(lo, ones_blk)


def _rope(x, cos, sin, lane_lo):
    outs = []
    for c in range(x.shape[1] // LANES):
        xc = x[:, c * LANES:(c + 1) * LANES]
        up = pltpu.roll(xc, LANES - 16, axis=1)
        dn = pltpu.roll(xc, 16, axis=1)
        partner = jnp.where(lane_lo, up, dn)
        outs.append(xc * cos + partner * sin)
    return jnp.concatenate(outs, axis=1)


def _inproj_kernel(h_ref, shift_ref, scale_ref, g_ref, w_ref, gq_ref, gk_ref, cos_ref, sin_ref,
                   ones_ref, sgun_ref, sguw_ref, sgub_ref,
                   q_ref, k_ref, v_ref, s_ref, sg_ref, w16_ref, *, rope):
    @pl.when((pl.program_id(0) == 0) & (pl.program_id(1) == 0))
    def _():
        w16_ref[...] = w_ref[...].astype(BF16)

    x = h_ref[0]
    a = _rms(x, g_ref[...]) * (1.0 + scale_ref[0]) + shift_ref[0]
    z = _dot(a.astype(BF16), w16_ref[...])
    tm = z.shape[0]

    q = z[:, _QO:_KO]
    k = z[:, _KO:_VO]
    ones_blk = ones_ref[...]
    qn = q * lax.rsqrt(_head_sumsq(q, ones_blk) * (1.0 / HEAD_DIM) + EPS) * gq_ref[...]
    kn = k * lax.rsqrt(_head_sumsq(k, ones_blk[:256, :256]) * (1.0 / HEAD_DIM) + EPS) * gk_ref[...]
    if rope:
        lane_lo = (lax.broadcasted_iota(jnp.int32, (tm, LANES), 1) % 32) < 16
        cos, sin = cos_ref[...], sin_ref[...]
        qn = _rope(qn, cos, sin, lane_lo)
        kn = _rope(kn, cos, sin, lane_lo)
    q_ref[0] = qn.astype(BF16)
    k_ref[0] = kn.astype(BF16)
    v_ref[0] = z[:, _VO:_SO].astype(BF16)
    s_ref[0] = z[:, _SO:_GO]

    gz = _gelu_tanh(z[:, _GO:])
    u = gz[:, :SGU_WIDTH]
    vv = _rms(gz[:, SGU_WIDTH:], sgun_ref[...]).astype(BF16)
    cg = SGU_WIDTH // SGU_GROUPS
    lane_grp = lax.broadcasted_iota(jnp.int32, (SGU_CHUNK, SGU_WIDTH), 1) // cg
    bias = sgub_ref[...]
    for ch in range(tm // SGU_CHUNK):
        vc = vv[ch * SGU_CHUNK:(ch + 1) * SGU_CHUNK, :]
        sp = bias
        for grp in range(SGU_GROUPS):
            mixed = _dot(sguw_ref[grp].astype(BF16), vc)
            sp = sp + jnp.where(lane_grp == grp, mixed, 0.0)
        sg_ref[0, ch * SGU_CHUNK:(ch + 1) * SGU_CHUNK, :] = u[ch * SGU_CHUNK:(ch + 1) * SGU_CHUNK, :] * sp


def _inproj_call(h, shift, scale, g, w_ext, gq, gk, cos_t, sin_t, ones_blk, sgun, sguw, sgub, *, rope):
    b, n, d = h.shape
    tm = min(ROW_TILE, n)
    assert n % tm == 0 and tm % SGU_CHUNK == 0
    row = lambda bb, i: (bb, i, 0)
    fixed2 = lambda bb, i: (0, 0)
    outs = (jax.ShapeDtypeStruct((b, n, ATTN_WIDTH), BF16),
            jax.ShapeDtypeStruct((b, n, 256), BF16),
            jax.ShapeDtypeStruct((b, n, 256), BF16),
            jax.ShapeDtypeStruct((b, n, S5_WIDTH), F32),
            jax.ShapeDtypeStruct((b, n, SGU_WIDTH), F32))
    return pl.pallas_call(
        functools.partial(_inproj_kernel, rope=rope),
        out_shape=outs,
        grid=(b, n // tm),
        in_specs=[pl.BlockSpec((1, tm, d), row),
                  pl.BlockSpec((1, 1, d), lambda bb, i: (bb, 0, 0)),
                  pl.BlockSpec((1, 1, d), lambda bb, i: (bb, 0, 0)),
                  pl.BlockSpec((1, d), fixed2),
                  pl.BlockSpec((d, IN_COLS), fixed2),
                  pl.BlockSpec((1, ATTN_WIDTH), fixed2),
                  pl.BlockSpec((1, 256), fixed2),
                  pl.BlockSpec((tm, LANES), lambda bb, i: (i, 0)),
                  pl.BlockSpec((tm, LANES), lambda bb, i: (i, 0)),
                  pl.BlockSpec((ATTN_WIDTH, ATTN_WIDTH), fixed2),
                  pl.BlockSpec((1, SGU_WIDTH), fixed2),
                  pl.BlockSpec((SGU_GROUPS, SGU_CHUNK, SGU_CHUNK), lambda bb, i: (0, 0, 0)),
                  pl.BlockSpec((SGU_CHUNK, SGU_WIDTH), fixed2)],
        out_specs=(pl.BlockSpec((1, tm, ATTN_WIDTH), row),
                   pl.BlockSpec((1, tm, 256), row),
                   pl.BlockSpec((1, tm, 256), row),
                   pl.BlockSpec((1, tm, S5_WIDTH), row),
                   pl.BlockSpec((1, tm, SGU_WIDTH), row)),
        scratch_shapes=[pltpu.VMEM((d, IN_COLS), BF16)],
        compiler_params=_cparams(2),
        name="inproj_rope" if rope else "inproj_ctx",
    )(h, shift, scale, g, w_ext, gq, gk, cos_t, sin_t, ones_blk, sgun, sguw, sgub)


def _stack_group(qj, kh, lane_half):
    pieces = []
    for g in range(KV_GROUP):
        head = kh * KV_GROUP + g
        qc = qj[:, (head // 2) * LANES:(head // 2 + 1) * LANES]
        pieces.append(jnp.where(lane_half == (head % 2), qc, jnp.zeros_like(qc)))
    return jnp.concatenate(pieces, axis=0)


def _sink_col(sink_ref, kh):
    rows = lax.broadcasted_iota(jnp.int32, (KV_GROUP * WINDOW, 1), 0) // WINDOW
    col = jnp.full((KV_GROUP * WINDOW, 1), sink_ref[kh * KV_GROUP], F32)
    for g in range(1, KV_GROUP):
        col = jnp.where(rows == g, sink_ref[kh * KV_GROUP + g], col)
    return col


def _unstack_store(o_ref, o, j, kh, lane_half_f):
    lo = lane_half_f == 0
    r = slice(j * WINDOW, (j + 1) * WINDOW)
    o_ref[0, r, (2 * kh) * LANES:(2 * kh + 1) * LANES] = jnp.where(lo, o[0:128], o[128:256])
    o_ref[0, r, (2 * kh + 1) * LANES:(2 * kh + 2) * LANES] = jnp.where(lo, o[256:384], o[384:512])


def _attn_win_kernel(sink_ref, q_ref, kc_ref, vc_ref, kp_ref, vp_ref, kn_ref, vn_ref, kx_ref, vx_ref, o_ref):
    i = pl.program_id(1)
    nt = pl.num_programs(1)
    tq = q_ref.shape[1]
    nsub = tq // WINDOW
    kfull = jnp.concatenate([kp_ref[0], kc_ref[0], kn_ref[0]], axis=0)
    vfull = jnp.concatenate([vp_ref[0], vc_ref[0], vn_ref[0]], axis=0)
    rows = KV_GROUP * WINDOW
    ri = lax.broadcasted_iota(jnp.int32, (rows, 3 * WINDOW), 0) % WINDOW
    ci = lax.broadcasted_iota(jnp.int32, (rows, 3 * WINDOW), 1)
    cblk = ci // WINDOW
    cj = ci % WINDOW
    ninf = np.float32(-np.inf)
    band = (cblk == 1) | ((cblk == 0) & (cj >= ri)) | ((cblk == 2) & (cj <= ri))
    bias0 = jnp.where(band, 0.0, ninf).astype(F32)
    no_prev = jnp.where(i > 0, 0.0, ninf).astype(F32)
    no_next = jnp.where(i < nt - 1, 0.0, ninf).astype(F32)
    lane_half = lax.broadcasted_iota(jnp.int32, (WINDOW, LANES), 1) // HEAD_DIM
    for j in range(nsub):
        qj = q_ref[0, j * WINDOW:(j + 1) * WINDOW, :]
        bias = bias0
        if j == 0:
            bias = bias + jnp.where(cblk == 0, no_prev, 0.0)
        if j == nsub - 1:
            bias = bias + jnp.where(cblk == 2, no_next, 0.0)
        for kh in range(N_KV_HEADS):
            qs = _stack_group(qj, kh, lane_half)
            ks = slice(kh * LANES, (kh + 1) * LANES)
            kw = kfull[j * WINDOW:(j + 3) * WINDOW, ks]
            vw = vfull[j * WINDOW:(j + 3) * WINDOW, ks]
            s_w = _dot_nt(qs, kw) + bias
            s_c = _dot_nt(qs, kx_ref[0, :, ks])
            sink = _sink_col(sink_ref, kh)
            m = jnp.maximum(jnp.maximum(s_w.max(-1, keepdims=True), s_c.max(-1, keepdims=True)), sink)
            e_w = jnp.exp(s_w - m)
            e_c = jnp.exp(s_c - m)
            den = e_w.sum(-1, keepdims=True) + e_c.sum(-1, keepdims=True) + jnp.exp(sink - m)
            o = _dot(e_w.astype(BF16), vw) + _dot(e_c.astype(BF16), vx_ref[0, :, ks])
            _unstack_store(o_ref, o / den, j, kh, lane_half)


def _attn_ctx_kernel(sink_ref, q_ref, kx_ref, vx_ref, o_ref):
    tq = q_ref.shape[1]
    lane_half = lax.broadcasted_iota(jnp.int32, (WINDOW, LANES), 1) // HEAD_DIM
    for j in range(tq // WINDOW):
        qj = q_ref[0, j * WINDOW:(j + 1) * WINDOW, :]
        for kh in range(N_KV_HEADS):
            qs = _stack_group(qj, kh, lane_half)
            ks = slice(kh * LANES, (kh + 1) * LANES)
            s_c = _dot_nt(qs, kx_ref[0, :, ks])
            sink = _sink_col(sink_ref, kh)
            m = jnp.maximum(s_c.max(-1, keepdims=True), sink)
            e_c = jnp.exp(s_c - m)
            den = e_c.sum(-1, keepdims=True) + jnp.exp(sink - m)
            o = _dot(e_c.astype(BF16), vx_ref[0, :, ks])
            _unstack_store(o_ref, o / den, j, kh, lane_half)


def _attn_win_call(sink, q, k, v, kx, vx):
    b, n, _ = q.shape
    lc = kx.shape[1]
    tq = min(ROW_TILE, n)
    nsub = tq // WINDOW
    nblk = n // WINDOW
    assert n % tq == 0 and tq % WINDOW == 0
    row = lambda bb, i: (bb, i, 0)
    prev = lambda bb, i: (bb, jnp.maximum(i * nsub - 1, 0), 0)
    nxt = lambda bb, i: (bb, jnp.minimum((i + 1) * nsub, nblk - 1), 0)
    ctx = lambda bb, i: (bb, 0, 0)
    return pl.pallas_call(
        _attn_win_kernel,
        out_shape=jax.ShapeDtypeStruct((b, n, ATTN_WIDTH), F32),
        grid=(b, n // tq),
        in_specs=[pl.BlockSpec(memory_space=pltpu.SMEM),
                  pl.BlockSpec((1, tq, ATTN_WIDTH), row),
                  pl.BlockSpec((1, tq, 256), row), pl.BlockSpec((1, tq, 256), row),
                  pl.BlockSpec((1, WINDOW, 256), prev), pl.BlockSpec((1, WINDOW, 256), prev),
                  pl.BlockSpec((1, WINDOW, 256), nxt), pl.BlockSpec((1, WINDOW, 256), nxt),
                  pl.BlockSpec((1, lc, 256), ctx), pl.BlockSpec((1, lc, 256), ctx)],
        out_specs=pl.BlockSpec((1, tq, ATTN_WIDTH), row),
        compiler_params=_cparams(2),
        name="attn_window",
    )(sink, q, k, v, k, v, k, v, kx, vx)


def _attn_ctx_call(sink, q, kx, vx):
    b, lc, _ = q.shape
    assert lc % WINDOW == 0
    blk = lambda bb: (bb, 0, 0)
    return pl.pallas_call(
        _attn_ctx_kernel,
        out_shape=jax.ShapeDtypeStruct((b, lc, ATTN_WIDTH), F32),
        grid=(b,),
        in_specs=[pl.BlockSpec(memory_space=pltpu.SMEM),
                  pl.BlockSpec((1, lc, ATTN_WIDTH), blk),
                  pl.BlockSpec((1, lc, 256), blk), pl.BlockSpec((1, lc, 256), blk)],
        out_specs=pl.BlockSpec((1, lc, ATTN_WIDTH), blk),
        compiler_params=_cparams(1),
        name="attn_ctx",
    )(sink, q, kx, vx)


def _s5_kernel(u_ref, perm_ref, permt_ref, bre_ref, bim_ref, a_ref, al_ref, cre_ref, cim_ref, h0_ref,
               y_ref, hfin_ref, bure_ref, buim_ref, hsre_ref, hsim_ref, carry_ref):
    t = u_ref.shape[1]
    seg_len = t // S5_SEGMENTS

    @pl.when(pl.program_id(1) == 0)
    def _():
        carry_ref[...] = h0_ref[0]

    up = _dot(perm_ref[...], u_ref[0].astype(BF16)).astype(BF16)
    bure_ref[...] = _dot(up, bre_ref[...])
    buim_ref[...] = _dot(up, bim_ref[...])
    a_re = jnp.broadcast_to(a_ref[0:1, :], (S5_SEGMENTS, S5_NSTATE))
    a_im = jnp.broadcast_to(a_ref[1:2, :], (S5_SEGMENTS, S5_NSTATE))

    def step(i, h):
        hr, hi = h
        r0 = pl.multiple_of(i * S5_SEGMENTS, S5_SEGMENTS)
        nr = a_re * hr - a_im * hi + bure_ref[pl.ds(r0, S5_SEGMENTS), :]
        ni = a_re * hi + a_im * hr + buim_ref[pl.ds(r0, S5_SEGMENTS), :]
        return nr, ni

    zero = jnp.zeros((S5_SEGMENTS, S5_NSTATE), F32)
    end_re, end_im = lax.fori_loop(0, seg_len, step, (zero, zero))

    al_re, al_im = al_ref[0:1, :], al_ref[1:2, :]
    c_re, c_im = carry_ref[0:1, :], carry_ref[1:2, :]
    in_re, in_im = [], []
    for s in range(S5_SEGMENTS):
        in_re.append(c_re)
        in_im.append(c_im)
        c_re, c_im = (al_re * c_re - al_im * c_im + end_re[s:s + 1, :],
                      al_re * c_im + al_im * c_re + end_im[s:s + 1, :])
    carry_ref[0:1, :] = c_re
    carry_ref[1:2, :] = c_im
    hfin_ref[0, 0:1, :] = c_re
    hfin_ref[0, 1:2, :] = c_im

    def step_store(i, h):
        nr, ni = step(i, h)
        r0 = pl.multiple_of(i * S5_SEGMENTS, S5_SEGMENTS)
        hsre_ref[pl.ds(r0, S5_SEGMENTS), :] = nr
        hsim_ref[pl.ds(r0, S5_SEGMENTS), :] = ni
        return nr, ni

    lax.fori_loop(0, seg_len, step_store, (jnp.concatenate(in_re, axis=0), jnp.concatenate(in_im, axis=0)))

    yp = _dot(hsre_ref[...].astype(BF16), cre_ref[...]) - _dot(hsim_ref[...].astype(BF16), cim_ref[...])
    yh, yl = _split(yp)
    y_ref[0] = _dot(permt_ref[...], yh) + _dot(permt_ref[...], yl)


def _s5_call(u, perm, permt, bre, bim, a, al, cre, cim, h0, *, reverse):
    b, n, w = u.shape
    t = perm.shape[0]
    nc = n // t
    assert n % t == 0
    if reverse:
        row = lambda bb, j: (bb, nc - 1 - j, 0)
    else:
        row = lambda bb, j: (bb, j, 0)
    fixed = lambda bb, j: (0, 0)
    return pl.pallas_call(
        _s5_kernel,
        out_shape=(jax.ShapeDtypeStruct((b, n, w), F32), jax.ShapeDtypeStruct((b, 2, S5_NSTATE), F32)),
        grid=(b, nc),
        in_specs=[pl.BlockSpec((1, t, w), row),
                  pl.BlockSpec((t, t), fixed), pl.BlockSpec((t, t), fixed),
                  pl.BlockSpec((w, S5_NSTATE), fixed), pl.BlockSpec((w, S5_NSTATE), fixed),
                  pl.BlockSpec((2, S5_NSTATE), fixed), pl.BlockSpec((2, S5_NSTATE), fixed),
                  pl.BlockSpec((S5_NSTATE, w), fixed), pl.BlockSpec((S5_NSTATE, w), fixed),
                  pl.BlockSpec((1, 2, S5_NSTATE), lambda bb, j: (bb, 0, 0))],
        out_specs=(pl.BlockSpec((1, t, w), row),
                   pl.BlockSpec((1, 2, S5_NSTATE), lambda bb, j: (bb, 0, 0))),
        scratch_shapes=[pltpu.VMEM((t, S5_NSTATE), F32)] * 4 + [pltpu.VMEM((2, S5_NSTATE), F32)],
        compiler_params=_cparams(2),
        name="s5_scan_rev" if reverse else "s5_scan_fwd",
    )(u, perm, permt, bre, bim, a, al, cre, cim, h0)


def _s5_perm(t, reverse):
    seg_len = t // S5_SEGMENTS
    r = np.arange(t)
    src = (r % S5_SEGMENTS) * seg_len + r // S5_SEGMENTS
    if reverse:
        src = t - 1 - src
    p = np.zeros((t, t), np.float32)
    p[r, src] = 1.0
    return jnp.asarray(p, BF16), jnp.asarray(p.T, BF16)


def _s5_params(lam_re, lam_im, log_dt, b_re, b_im, c_re, c_im, seg_len):
    dt = jnp.exp(log_dt.astype(F32))[:, None]
    lr, li = lam_re.astype(F32), lam_im.astype(F32)
    mag = jnp.exp(lr * dt)
    ang = li * dt
    ab_re, ab_im = mag * jnp.cos(ang), mag * jnp.sin(ang)
    nr, ni = ab_re - 1.0, ab_im
    den = lr * lr + li * li
    coef_re = (nr * lr + ni * li) / den
    coef_im = (ni * lr - nr * li) / den
    br, bi = b_re.astype(F32), b_im.astype(F32)
    bb_re = coef_re[..., None] * br - coef_im[..., None] * bi
    bb_im = coef_re[..., None] * bi + coef_im[..., None] * br
    eye = jnp.eye(S5_GROUPS, dtype=F32)

    def in_mat(bb):
        return jnp.einsum("gpc,gh->gchp", bb, eye).reshape(S5_WIDTH, S5_NSTATE).astype(BF16)

    def out_mat(cc):
        return jnp.einsum("gcp,gh->gphc", cc.astype(F32), eye).reshape(S5_NSTATE, S5_WIDTH).astype(BF16)

    a = jnp.stack([ab_re.reshape(-1), ab_im.reshape(-1)])
    pr, pi = ab_re, ab_im
    steps = int(np.log2(seg_len))
    assert 2 ** steps == seg_len
    for _ in range(steps):
        pr, pi = pr * pr - pi * pi, 2.0 * pr * pi
    al = jnp.stack([pr.reshape(-1), pi.reshape(-1)])
    return in_mat(bb_re), in_mat(bb_im), a, al, out_mat(c_re), out_mat(c_im)


def _mix_kernel(h_ref, att_ref, yf_ref, yb_ref, us_ref, sg_ref, gate_ref, shift_ref, scale_ref,
                dskip_ref, gluw_ref, glub_ref, outg_ref, wout_ref, gffn_ref, rw_ref,
                hnew_ref, f_ref, p_ref, w16_ref):
    @pl.when((pl.program_id(0) == 0) & (pl.program_id(1) == 0))
    def _():
        w16_ref[...] = wout_ref[...].astype(BF16)

    y = yf_ref[0] + yb_ref[0] + dskip_ref[...] * us_ref[0]
    g = _gelu_tanh(y)
    s5 = g * _sigmoid(_dot(g.astype(BF16), gluw_ref[...]) + glub_ref[...])
    outg = outg_ref[...]
    o1, o2 = ATTN_WIDTH, ATTN_WIDTH + S5_WIDTH
    pa = _rms(att_ref[0], outg[:, :o1]).astype(BF16)
    ps = _rms(s5, outg[:, o1:o2]).astype(BF16)
    pg = _rms(sg_ref[0], outg[:, o2:]).astype(BF16)
    mixed = _dot(pa, w16_ref[:o1, :]) + _dot(ps, w16_ref[o1:o2, :]) + _dot(pg, w16_ref[o2:, :])
    hn = h_ref[0] + gate_ref[0] * mixed
    hnew_ref[0] = hn

    f = _rms(hn, gffn_ref[...]) * (1.0 + scale_ref[0]) + shift_ref[0]
    fh, fl = _split(f)
    f_ref[0] = fh
    rh, rl = _split(rw_ref[...])
    logits = _dot_nt(rh, fh) + _dot_nt(rh, fl) + _dot_nt(rl, fh)
    m = logits.max(axis=0, keepdims=True)
    e = jnp.exp(logits - m)
    p_ref[0] = e / e.sum(axis=0, keepdims=True)


def _mix_call(h, att, yf, yb, us, sg, gate, shift, scale, dskip, gluw, glub, outg, wout, gffn, rwt):
    b, n, d = h.shape
    tm = min(ROW_TILE, n)
    assert n % tm == 0
    row = lambda bb, i: (bb, i, 0)
    vec = lambda bb, i: (bb, 0, 0)
    fixed = lambda bb, i: (0, 0)
    return pl.pallas_call(
        _mix_kernel,
        out_shape=(jax.ShapeDtypeStruct((b, n, d), F32),
                   jax.ShapeDtypeStruct((b, n, d), BF16),
                   jax.ShapeDtypeStruct((b, N_EXPERTS, n), F32)),
        grid=(b, n // tm),
        in_specs=[pl.BlockSpec((1, tm, d), row),
                  pl.BlockSpec((1, tm, ATTN_WIDTH), row),
                  pl.BlockSpec((1, tm, S5_WIDTH), row), pl.BlockSpec((1, tm, S5_WIDTH), row),
                  pl.BlockSpec((1, tm, S5_WIDTH), row), pl.BlockSpec((1, tm, SGU_WIDTH), row),
                  pl.BlockSpec((1, 1, d), vec), pl.BlockSpec((1, 1, d), vec), pl.BlockSpec((1, 1, d), vec),
                  pl.BlockSpec((1, S5_WIDTH), fixed),
                  pl.BlockSpec((S5_WIDTH, S5_WIDTH), fixed),
                  pl.BlockSpec((1, S5_WIDTH), fixed),
                  pl.BlockSpec((1, d), fixed),
                  pl.BlockSpec((d, d), fixed),
                  pl.BlockSpec((1, d), fixed),
                  pl.BlockSpec((N_EXPERTS, d), fixed)],
        out_specs=(pl.BlockSpec((1, tm, d), row), pl.BlockSpec((1, tm, d), row),
                   pl.BlockSpec((1, N_EXPERTS, tm), lambda bb, i: (bb, 0, i))),
        scratch_shapes=[pltpu.VMEM((d, d), BF16)],
        compiler_params=_cparams(2),
        name="mix_out_router",
    )(h, att, yf, yb, us, sg, gate, shift, scale, dskip, gluw, glub, outg, wout, gffn, rwt)


def _thr_kernel(p_ref, thr_ref, allow_ref, *, cap):
    bits = pltpu.bitcast(p_ref[0], jnp.int32)

    def count(mask):
        return jnp.sum(mask.astype(F32), axis=1, keepdims=True)

    def body(it, thr):
        cand = thr | jnp.left_shift(jnp.int32(1), 30 - it)
        return jnp.where(count(bits >= cand) >= cap, cand, thr)

    thr = lax.fori_loop(0, 31, body, jnp.zeros((N_EXPERTS, 1), jnp.int32))
    allow = (cap - count(bits > thr)).astype(jnp.int32)
    thr_ref[0] = jnp.broadcast_to(thr, (N_EXPERTS, LANES))
    allow_ref[0] = jnp.broadcast_to(allow, (N_EXPERTS, LANES))


def _thr_call(probs, cap):
    b, e, n = probs.shape
    blk = lambda bb: (bb, 0, 0)
    return pl.pallas_call(
        functools.partial(_thr_kernel, cap=cap),
        out_shape=(jax.ShapeDtypeStruct((b, e, LANES), jnp.int32),) * 2,
        grid=(b,),
        in_specs=[pl.BlockSpec((1, e, n), blk)],
        out_specs=(pl.BlockSpec((1, e, LANES), blk),) * 2,
        compiler_params=_cparams(1),
        name="route_threshold",
    )(probs)


def _rank_kernel(p_ref, thr_ref, allow_ref, rank_ref, offs_ref, eqseen_ref, selseen_ref):
    @pl.when(pl.program_id(1) == 0)
    def _():
        eqseen_ref[...] = jnp.zeros_like(eqseen_ref)
        selseen_ref[...] = jnp.zeros_like(selseen_ref)

    tt = p_ref.shape[2]
    bits = pltpu.bitcast(p_ref[0], jnp.int32)
    thr = thr_ref[0][:, 0:1]
    allow = allow_ref[0][:, 0:1].astype(F32)
    gt = bits > thr
    eq = bits == thr
    before = (lax.broadcasted_iota(jnp.int32, (tt, tt), 0) <
              lax.broadcasted_iota(jnp.int32, (tt, tt), 1)).astype(F32).astype(BF16)
    eq_seen = eqseen_ref[...][:, 0:1]
    eqf = eq.astype(F32)
    eq_before = eq_seen + _dot(eqf.astype(BF16), before)
    sel = gt | (eq & (eq_before < allow))
    self32 = sel.astype(F32)
    sel_seen = selseen_ref[...][:, 0:1]
    rank = sel_seen + _dot(self32.astype(BF16), before)
    rank_ref[0] = jnp.where(sel, rank, -1.0)
    offs_ref[0, 0] = jnp.broadcast_to(sel_seen, (N_EXPERTS, LANES)).astype(jnp.int32)
    eqseen_ref[...] = jnp.broadcast_to(eq_seen + eqf.sum(axis=1, keepdims=True), (N_EXPERTS, LANES))
    selseen_ref[...] = jnp.broadcast_to(sel_seen + self32.sum(axis=1, keepdims=True), (N_EXPERTS, LANES))


def _rank_call(probs, thr, allow, tt):
    b, e, n = probs.shape
    nt = n // tt
    blk = lambda bb, t: (bb, 0, 0)
    return pl.pallas_call(
        _rank_kernel,
        out_shape=(jax.ShapeDtypeStruct((b, e, n), F32),
                   jax.ShapeDtypeStruct((b, nt, e, LANES), jnp.int32)),
        grid=(b, nt),
        in_specs=[pl.BlockSpec((1, e, tt), lambda bb, t: (bb, 0, t)),
                  pl.BlockSpec((1, e, LANES), blk), pl.BlockSpec((1, e, LANES), blk)],
        out_specs=(pl.BlockSpec((1, e, tt), lambda bb, t: (bb, 0, t)),
                   pl.BlockSpec((1, 1, e, LANES), lambda bb, t: (bb, t, 0, 0))),
        scratch_shapes=[pltpu.VMEM((e, LANES), F32)] * 2,
        compiler_params=_cparams(2),
        name="route_rank",
    )(probs, thr, allow)


def _route_items(offs, cap, rb):
    b, e, nt = offs.shape
    nd = cap // rb
    blk_starts = jnp.broadcast_to(jnp.arange(nd, dtype=jnp.int32) * rb, (b, e, nd))
    s = jnp.sort(jnp.concatenate([offs, blk_starts], axis=-1), axis=-1)
    nxt = jnp.concatenate([s[..., 1:], jnp.full((b, e, 1), cap, jnp.int32)], axis=-1)
    valid = (nxt > s).astype(jnp.int32)
    t = jnp.sum((offs[..., None, :] <= s[..., :, None]).astype(jnp.int32), axis=-1) - 1
    d = jnp.minimum(s // rb, nd - 1)
    return t, d, valid


def _onehot_rows(rank_row, d, rb):
    tt = rank_row.shape[1]
    rel = rank_row - (d * rb).astype(F32)
    return rel == lax.broadcasted_iota(jnp.int32, (rb, tt), 0).astype(F32)


def _gather_kernel(t_sm, d_sm, v_sm, f_sm, rank_ref, p_ref, x_ref, xin_ref, gsel_ref, acc_ref, gacc_ref):
    bb, e, w = pl.program_id(0), pl.program_id(1), pl.program_id(2)
    idx = (bb * pl.num_programs(1) + e) * pl.num_programs(2) + w
    rb = acc_ref.shape[0]

    @pl.when(f_sm[idx] == 1)
    def _():
        acc_ref[...] = jnp.zeros_like(acc_ref)
        gacc_ref[...] = jnp.zeros_like(gacc_ref)

    @pl.when(v_sm[idx] == 1)
    def _():
        hit = _onehot_rows(rank_ref[0, 0], d_sm[idx], rb)
        acc_ref[...] += _dot(hit.astype(F32).astype(BF16), x_ref[0])
        gacc_ref[...] += jnp.sum(jnp.where(hit, p_ref[0, 0], 0.0), axis=1, keepdims=True)

    xin_ref[0, 0] = acc_ref[...].astype(BF16)
    gsel_ref[0, 0] = jnp.broadcast_to(gacc_ref[...], (rb, LANES))


def _gather_call(t_w, d_w, v_w, f_w, rank4, probs4, f16, cap, rb, tt):
    b, n, d = f16.shape
    e = N_EXPERTS
    w = t_w.shape[0] // (b * e)

    def item(bb, ee, ww):
        return (bb * e + ee) * w + ww

    grid_spec = pltpu.PrefetchScalarGridSpec(
        num_scalar_prefetch=4,
        grid=(b, e, w),
        in_specs=[pl.BlockSpec((1, 1, 1, tt), lambda bb, ee, ww, ts, ds, vs, fs: (bb, ee, 0, ts[item(bb, ee, ww)])),
                  pl.BlockSpec((1, 1, 1, tt), lambda bb, ee, ww, ts, ds, vs, fs: (bb, ee, 0, ts[item(bb, ee, ww)])),
                  pl.BlockSpec((1, tt, d), lambda bb, ee, ww, ts, ds, vs, fs: (bb, ts[item(bb, ee, ww)], 0))],
        out_specs=(pl.BlockSpec((1, 1, rb, d), lambda bb, ee, ww, ts, ds, vs, fs: (bb, ee, ds[item(bb, ee, ww)], 0)),
                   pl.BlockSpec((1, 1, rb, LANES),
                                lambda bb, ee, ww, ts, ds, vs, fs: (bb, ee, ds[item(bb, ee, ww)], 0))),
        scratch_shapes=[pltpu.VMEM((rb, d), F32), pltpu.VMEM((rb, 1), F32)],
    )
    return pl.pallas_call(
        _gather_kernel,
        out_shape=(jax.ShapeDtypeStruct((b, e, cap, d), BF16),
                   jax.ShapeDtypeStruct((b, e, cap, LANES), F32)),
        grid_spec=grid_spec,
        compiler_params=_cparams(3),
        name="moe_gather",
    )(t_w, d_w, v_w, f_w, rank4, probs4, f16)


def _ffn_kernel(x_ref, g_ref, wg_ref, wu_ref, wd_ref, y_ref, wg16, wu16, wd16):
    @pl.when((pl.program_id(1) == 0) & (pl.program_id(2) == 0))
    def _():
        wg16[...] = wg_ref[0].astype(BF16)
        wu16[...] = wu_ref[0].astype(BF16)
        wd16[...] = wd_ref[0].astype(BF16)

    x = x_ref[0, 0]
    hg = _dot(x, wg16[...])
    hid = (hg * _sigmoid(hg)) * _dot(x, wu16[...])
    y = _dot(hid.astype(BF16), wd16[...]) * g_ref[0, 0][:, 0:1]
    y_ref[0, 0] = y.astype(BF16)


def _ffn_call(xin, gsel, wg, wu, wd):
    b, e, cap, d = xin.shape
    ff = wg.shape[2]
    tm = min(ROW_TILE, cap)
    assert cap % tm == 0
    xrow = lambda ee, bb, m: (bb, ee, m, 0)
    wblk = lambda ee, bb, m: (ee, 0, 0)
    return pl.pallas_call(
        _ffn_kernel,
        out_shape=jax.ShapeDtypeStruct((b, e, cap, d), BF16),
        grid=(e, b, cap // tm),
        in_specs=[pl.BlockSpec((1, 1, tm, d), xrow),
                  pl.BlockSpec((1, 1, tm, LANES), xrow),
                  pl.BlockSpec((1, d, ff), wblk), pl.BlockSpec((1, d, ff), wblk), pl.BlockSpec((1, ff, d), wblk)],
        out_specs=pl.BlockSpec((1, 1, tm, d), xrow),
        scratch_shapes=[pltpu.VMEM((d, ff), BF16), pltpu.VMEM((d, ff), BF16), pltpu.VMEM((ff, d), BF16)],
        compiler_params=_cparams(3),
        name="moe_expert_ffn",
    )(xin, gsel, wg, wu, wd)


def _combine_kernel(t_sm, e_sm, d_sm, v_sm, first_sm, last_sm, rank_ref, y_ref, h_ref, gate_ref,
                    o_ref, acc_ref):
    bb, w = pl.program_id(0), pl.program_id(1)
    idx = bb * pl.num_programs(1) + w
    rb = y_ref.shape[2]

    @pl.when(first_sm[idx] == 1)
    def _():
        acc_ref[...] = jnp.zeros_like(acc_ref)

    @pl.when(v_sm[idx] == 1)
    def _():
        hit = _onehot_rows(rank_ref[0, 0], d_sm[idx], rb)
        acc_ref[...] += _dot_tn(hit.astype(F32).astype(BF16), y_ref[0, 0])

    @pl.when(last_sm[idx] == 1)
    def _():
        o_ref[0] = h_ref[0] + gate_ref[0] * acc_ref[...]


def _combine_call(lists, rank4, y, h, gate, rb, tt):
    b, n, d = h.shape
    t_w, e_w, d_w, v_w, first_w, last_w = lists
    w = t_w.shape[0] // b

    def item(bb, ww):
        return bb * w + ww

    grid_spec = pltpu.PrefetchScalarGridSpec(
        num_scalar_prefetch=6,
        grid=(b, w),
        in_specs=[pl.BlockSpec((1, 1, 1, tt), lambda bb, ww, ts, es, ds, vs, fs, ls:
                               (bb, es[item(bb, ww)], 0, ts[item(bb, ww)])),
                  pl.BlockSpec((1, 1, rb, d), lambda bb, ww, ts, es, ds, vs, fs, ls:
                               (bb, es[item(bb, ww)], ds[item(bb, ww)], 0)),
                  pl.BlockSpec((1, tt, d), lambda bb, ww, ts, es, ds, vs, fs, ls: (bb, ts[item(bb, ww)], 0)),
                  pl.BlockSpec((1, 1, d), lambda bb, ww, ts, es, ds, vs, fs, ls: (bb, 0, 0))],
        out_specs=pl.BlockSpec((1, tt, d), lambda bb, ww, ts, es, ds, vs, fs, ls: (bb, ts[item(bb, ww)], 0)),
        scratch_shapes=[pltpu.VMEM((tt, d), F32)],
    )
    return pl.pallas_call(
        _combine_kernel,
        out_shape=jax.ShapeDtypeStruct((b, n, d), F32),
        grid_spec=grid_spec,
        compiler_params=_cparams(2),
        name="moe_combine",
    )(t_w, e_w, d_w, v_w, first_w, last_w, rank4, y, h, gate)


def _moe(h, f16, probs, gate, wg, wu, wd):
    b, n, d = h.shape
    e = N_EXPERTS
    cap = EC_CAPACITY_FACTOR * n // e
    rb = min(ROUTE_BLOCK, cap)
    tt = min(ROW_TILE, n)
    nt, nd = n // tt, cap // rb
    thr, allow = _thr_call(probs, cap)
    rank, offs = _rank_call(probs, thr, allow, tt)
    offs = jnp.transpose(offs[..., 0], (0, 2, 1))
    t_w, d_w, v_w = _route_items(offs, cap, rb)
    w = nt + nd
    first_w = jnp.concatenate([jnp.ones((b, e, 1), jnp.int32),
                               (d_w[..., 1:] != d_w[..., :-1]).astype(jnp.int32)], axis=-1)
    rank4 = rank.reshape(b, e, 1, n)
    probs4 = probs.reshape(b, e, 1, n)
    flat = lambda a: a.reshape(-1).astype(jnp.int32)
    xin, gsel = _gather_call(flat(t_w), flat(d_w), flat(v_w), flat(first_w), rank4, probs4, f16, cap, rb, tt)
    y = _ffn_call(xin, gsel, wg, wu, wd)

    e_w = jnp.broadcast_to(jnp.arange(e, dtype=jnp.int32)[None, :, None], (b, e, w))
    ct = jnp.concatenate([t_w.reshape(b, e * w), jnp.broadcast_to(jnp.arange(nt, dtype=jnp.int32), (b, nt))], axis=1)
    pad = jnp.zeros((b, nt), jnp.int32)
    ce = jnp.concatenate([e_w.reshape(b, e * w), pad], axis=1)
    cd = jnp.concatenate([d_w.reshape(b, e * w), pad], axis=1)
    cv = jnp.concatenate([v_w.reshape(b, e * w), pad], axis=1)
    order = jnp.argsort(ct, axis=1, stable=True)
    take = lambda a: jnp.take_along_axis(a, order, axis=1)
    ct, ce, cd, cv = take(ct), take(ce), take(cd), take(cv)
    edge = jnp.ones((b, 1), jnp.int32)
    cfirst = jnp.concatenate([edge, (ct[:, 1:] != ct[:, :-1]).astype(jnp.int32)], axis=1)
    clast = jnp.concatenate([(ct[:, 1:] != ct[:, :-1]).astype(jnp.int32), edge], axis=1)
    lists = tuple(flat(a) for a in (ct, ce, cd, cv, cfirst, clast))
    return _combine_call(lists, rank4, y, h, gate, rb, tt)


def _rope_tables(n):
    rows = n // GRID_W
    row = jnp.repeat(jnp.arange(rows), GRID_W).astype(F32)
    col = jnp.tile(jnp.arange(GRID_W), rows).astype(F32)
    axis_dim = HEAD_DIM // 2
    inv = ROPE_BASE ** (-jnp.arange(0, axis_dim, 2, dtype=F32) / axis_dim)
    ang_r = row[:, None] * inv
    ang_c = col[:, None] * inv
    cr, sr, cc, sc = jnp.cos(ang_r), jnp.sin(ang_r), jnp.cos(ang_c), jnp.sin(ang_c)
    cos64 = jnp.concatenate([cr, cr, cc, cc], axis=1)
    sin64 = jnp.concatenate([-sr, sr, -sc, sc], axis=1)
    return jnp.tile(cos64, (1, 2)), jnp.tile(sin64, (1, 2))


def _extend_w_in(w_in):
    q = w_in[:, :512]
    k = w_in[:, 512:640]
    v = w_in[:, 640:768]
    rest = w_in[:, 768:]
    dup = lambda m: jnp.concatenate([m[:, :64], m[:, :64], m[:, 64:], m[:, 64:]], axis=1)
    return jnp.concatenate([q, dup(k), dup(v), rest], axis=1)


def kernel(x, c, ctx, c_ctx, ada_w, ada_b, norm_mix_g, norm_ffn_g, w_in, q_norm_g, k_norm_g, attn_sink,
           s5_lambda_re, s5_lambda_im, s5_log_dt, s5_b_re, s5_b_im, s5_c_re, s5_c_im, s5_d, s5_glu_w,
           s5_glu_b, sgu_norm_g, sgu_w, sgu_b, out_norm_g, w_out, router_w, exp_w_gate, exp_w_up,
           exp_w_down):
    b, n, d = x.shape
    lc = ctx.shape[1]
    depth = ada_w.shape[0]
    assert b + 1 <= SUBLANES

    crows = jnp.zeros((SUBLANES, d), F32).at[:b].set(c).at[b].set(c_ctx)
    mod = _ada_call(crows, ada_w, ada_b)

    cos_t, sin_t = _rope_tables(n)
    cos_c, sin_c = cos_t[:lc], sin_t[:lc]
    ones_blk = jnp.asarray(np.arange(ATTN_WIDTH)[:, None] // HEAD_DIM == np.arange(ATTN_WIDTH)[None, :] // HEAD_DIM,
                           BF16)
    t_lat = min(ROW_TILE, n)
    t_ctx = min(ROW_TILE, lc)
    perms = {(t, r): _s5_perm(t, r) for t in {t_lat, t_ctx} for r in (False, True)}

    h_lat, h_ctx = x, ctx
    for l in range(depth):
        ctx_out = l < depth - 1
        mod_lat = mod[l, :b].reshape(b, N_MOD, 1, d)
        mod_ctx = jnp.broadcast_to(mod[l, b].reshape(1, N_MOD, 1, d), (b, N_MOD, 1, d))
        row = lambda v: v.reshape(1, -1)

        w_ext = _extend_w_in(w_in[l])
        gq = row(jnp.tile(q_norm_g[l], N_HEADS) * np.float32(HEAD_DIM ** -0.5))
        gk = row(jnp.tile(k_norm_g[l], 2 * N_KV_HEADS))
        sgub = jnp.repeat(sgu_b[l].T, SGU_WIDTH // SGU_GROUPS, axis=1)
        common = (row(norm_mix_g[l]), w_ext, gq, gk)
        sgu_args = (ones_blk, row(sgu_norm_g[l]), sgu_w[l], sgub)
        q_l, k_l, v_l, s_l, sg_l = _inproj_call(h_lat, mod_lat[:, 0], mod_lat[:, 1], *common, cos_t, sin_t,
                                                *sgu_args, rope=True)
        q_c, k_c, v_c, s_c, sg_c = _inproj_call(h_ctx, mod_ctx[:, 0], mod_ctx[:, 1], *common, cos_c, sin_c,
                                                *sgu_args, rope=False)

        att_lat = _attn_win_call(attn_sink[l], q_l, k_l, v_l, k_c, v_c)

        ys_lat, ys_ctx = [], []
        for direction in range(2):
            rev = direction == 1
            seg_lat, seg_ctx = t_lat // S5_SEGMENTS, t_ctx // S5_SEGMENTS
            pa = (s5_lambda_re[l, direction], s5_lambda_im[l, direction], s5_log_dt[l, direction],
                  s5_b_re[l, direction], s5_b_im[l, direction], s5_c_re[l, direction], s5_c_im[l, direction])
            bre, bim, a, al_lat, cre, cim = _s5_params(*pa, seg_lat)
            al_ctx = _s5_params(*pa, seg_ctx)[3]
            h0 = jnp.zeros((b, 2, S5_NSTATE), F32)
            y_c, hfin = _s5_call(s_c, *perms[(t_ctx, rev)], bre, bim, a, al_ctx, cre, cim, h0, reverse=rev)
            y_l, _ = _s5_call(s_l, *perms[(t_lat, rev)], bre, bim, a, al_lat, cre, cim, hfin, reverse=rev)
            ys_lat.append(y_l)
            ys_ctx.append(y_c)

        tail = (row(s5_d[l]), s5_glu_w[l].astype(BF16), row(s5_glu_b[l]), row(out_norm_g[l]), w_out[l],
                row(norm_ffn_g[l]), router_w[l].T)
        h_mid, f_lat, p_lat = _mix_call(h_lat, att_lat, ys_lat[0], ys_lat[1], s_l, sg_l,
                                        mod_lat[:, 2], mod_lat[:, 3], mod_lat[:, 4], *tail)
        h_lat = _moe(h_mid, f_lat, p_lat, mod_lat[:, 5], exp_w_gate[l], exp_w_up[l], exp_w_down[l])
        if ctx_out:
            att_ctx = _attn_ctx_call(attn_sink[l], q_c, k_c, v_c)
            hc_mid, f_ctx, p_ctx = _mix_call(h_ctx, att_ctx, ys_ctx[0], ys_ctx[1], s_c, sg_c,
                                             mod_ctx[:, 2], mod_ctx[:, 3], mod_ctx[:, 4], *tail)
            h_ctx = _moe(hc_mid, f_ctx, p_ctx, mod_ctx[:, 5], exp_w_gate[l], exp_w_up[l], exp_w_down[l])
    return h_lat
```

```python
import functools

import numpy as np
import jax
import jax.numpy as jnp
from jax import lax
from jax.experimental import pallas as pl
from jax.experimental.pallas import tpu as pltpu

F32 = jnp.float32
BF16 = jnp.bfloat16

HEAD_DIM = 64
N_HEADS = 8
N_KV_HEADS = 2
KV_GROUP = N_HEADS // N_KV_HEADS
ATTN_WIDTH = N_HEADS * HEAD_DIM
WINDOW = 128
GRID_W = 64
ROPE_BASE = 10000.0
S5_WIDTH = 256
S5_GROUP = 16
S5_GROUPS = S5_WIDTH // S5_GROUP
S5_STATE = 64
S5_NSTATE = S5_GROUPS * S5_STATE
SGU_WIDTH = 256
SGU_GROUPS = 4
SGU_CHUNK = 128
N_EXPERTS = 16
EC_CAPACITY_FACTOR = 2
N_MOD = 6
EPS = 1e-6

LANES = 128
SUBLANES = 8
VMEM_LIMIT_BYTES = 56 * 1024 * 1024

ROW_TILE = 512
S5_SEGMENTS = SUBLANES
ROUTE_BLOCK = 128
ROW_ALIGN = 16
MOE_GROUP = 4


def _cparams(n_axes):
    return pltpu.CompilerParams(dimension_semantics=("arbitrary",) * n_axes,
                                vmem_limit_bytes=VMEM_LIMIT_BYTES)


def _dot(a, b):
    return jnp.dot(a, b, preferred_element_type=F32)


def _dot_nt(a, b):
    return lax.dot_general(a, b, (((1,), (1,)), ((), ())), preferred_element_type=F32)


def _dot_tn(a, b):
    return lax.dot_general(a, b, (((0,), (0,)), ((), ())), preferred_element_type=F32)


def _split(x):
    hi = x.astype(BF16)
    lo = (x - hi.astype(F32)).astype(BF16)
    return hi, lo


def _sigmoid(x):
    return 1.0 / (1.0 + jnp.exp(-x))


def _gelu_tanh(x):
    c = np.float32(np.sqrt(2.0 / np.pi))
    return x * (0.5 * (1.0 + jnp.tanh(c * (x + np.float32(0.044715) * (x * x * x)))))


def _rms(x, g):
    return x * lax.rsqrt(jnp.mean(x * x, axis=-1, keepdims=True) + EPS) * g


def _ada_kernel(c_ref, w_ref, b_ref, o_ref):
    c = c_ref[...]
    sc = c * _sigmoid(c)
    ch, cl = _split(sc)
    wh, wl = _split(w_ref[0])
    o_ref[0] = _dot(ch, wh) + _dot(ch, wl) + _dot(cl, wh) + b_ref[0]


def _ada_call(crows, ada_w, ada_b):
    depth, d, nmod = ada_w.shape
    rows = crows.shape[0]
    tn = 1536
    assert nmod % tn == 0
    return pl.pallas_call(
        _ada_kernel,
        out_shape=jax.ShapeDtypeStruct((depth, rows, nmod), F32),
        grid=(depth, nmod // tn),
        in_specs=[pl.BlockSpec((rows, d), lambda l, j: (0, 0)),
                  pl.BlockSpec((1, d, tn), lambda l, j: (l, 0, j)),
                  pl.BlockSpec((1, 1, tn), lambda l, j: (l, 0, j))],
        out_specs=pl.BlockSpec((1, rows, tn), lambda l, j: (l, 0, j)),
        compiler_params=_cparams(2),
        name="ada_mod",
    )(crows, ada_w, ada_b.reshape(depth, 1, nmod))


IN_COLS = ATTN_WIDTH + 2 * (2 * N_KV_HEADS * HEAD_DIM) + S5_WIDTH + 2 * SGU_WIDTH
_QO, _KO, _VO, _SO, _GO = 0, 512, 768, 1024, 1280


def _head_sumsq(x, ones_blk):
    hi, lo = _split(x * x)
    return _dot(hi, ones_blk) + _dot(lo, ones_blk)


def _rope(x, cos, sin, lane_lo):
    outs = []
    for c in range(x.shape[1] // LANES):
        xc = x[:, c * LANES:(c + 1) * LANES]
        up = pltpu.roll(xc, LANES - 16, axis=1)
        dn = pltpu.roll(xc, 16, axis=1)
        partner = jnp.where(lane_lo, up, dn)
        outs.append(xc * cos + partner * sin)
    return jnp.concatenate(outs, axis=1)


def _inproj_kernel(h_ref, shift_ref, scale_ref, g_ref, w_ref, gq_ref, gk_ref, cos_ref, sin_ref,
                   ones_ref, sgun_ref, sguw_ref, sgub_ref,
                   q_ref, k_ref, v_ref, s_ref, sg_ref, w16_ref, *, rope):
    @pl.when((pl.program_id(0) == 0) & (pl.program_id(1) == 0))
    def _():
        w16_ref[...] = w_ref[...].astype(BF16)

    x = h_ref[0]
    a = _rms(x, g_ref[...]) * (1.0 + scale_ref[0]) + shift_ref[0]
    z = _dot(a.astype(BF16), w16_ref[...])
    tm = z.shape[0]

    q = z[:, _QO:_KO]
    k = z[:, _KO:_VO]
    ones_blk = ones_ref[...]
    qn = q * lax.rsqrt(_head_sumsq(q, ones_blk) * (1.0 / HEAD_DIM) + EPS) * gq_ref[...]
    kn = k * lax.rsqrt(_head_sumsq(k, ones_blk[:256, :256]) * (1.0 / HEAD_DIM) + EPS) * gk_ref[...]
    if rope:
        lane_lo = (lax.broadcasted_iota(jnp.int32, (tm, LANES), 1) % 32) < 16
        cos, sin = cos_ref[...], sin_ref[...]
        qn = _rope(qn, cos, sin, lane_lo)
        kn = _rope(kn, cos, sin, lane_lo)
    q_ref[0] = qn.astype(BF16)
    k_ref[0] = kn.astype(BF16)
    v_ref[0] = z[:, _VO:_SO].astype(BF16)
    s_ref[0] = z[:, _SO:_GO]

    gz = _gelu_tanh(z[:, _GO:])
    u = gz[:, :SGU_WIDTH]
    vv = _rms(gz[:, SGU_WIDTH:], sgun_ref[...]).astype(BF16)
    cg = SGU_WIDTH // SGU_GROUPS
    lane_grp = lax.broadcasted_iota(jnp.int32, (SGU_CHUNK, SGU_WIDTH), 1) // cg
    bias = sgub_ref[...]
    for ch in range(tm // SGU_CHUNK):
        vc = vv[ch * SGU_CHUNK:(ch + 1) * SGU_CHUNK, :]
        sp = bias
        for grp in range(SGU_GROUPS):
            mixed = _dot(sguw_ref[grp].astype(BF16), vc)
            sp = sp + jnp.where(lane_grp == grp, mixed, 0.0)
        sg_ref[0, ch * SGU_CHUNK:(ch + 1) * SGU_CHUNK, :] = u[ch * SGU_CHUNK:(ch + 1) * SGU_CHUNK, :] * sp


def _inproj_call(h, shift, scale, g, w_ext, gq, gk, cos_t, sin_t, ones_blk, sgun, sguw, sgub, *, rope):
    b, n, d = h.shape
    tm = min(ROW_TILE, n)
    assert n % tm == 0 and tm % SGU_CHUNK == 0
    row = lambda bb, i: (bb, i, 0)
    fixed2 = lambda bb, i: (0, 0)
    outs = (jax.ShapeDtypeStruct((b, n, ATTN_WIDTH), BF16),
            jax.ShapeDtypeStruct((b, n, 256), BF16),
            jax.ShapeDtypeStruct((b, n, 256), BF16),
            jax.ShapeDtypeStruct((b, n, S5_WIDTH), F32),
            jax.ShapeDtypeStruct((b, n, SGU_WIDTH), F32))
    return pl.pallas_call(
        functools.partial(_inproj_kernel, rope=rope),
        out_shape=outs,
        grid=(b, n // tm),
        in_specs=[pl.BlockSpec((1, tm, d), row),
                  pl.BlockSpec((1, 1, d), lambda bb, i: (bb, 0, 0)),
                  pl.BlockSpec((1, 1, d), lambda bb, i: (bb, 0, 0)),
                  pl.BlockSpec((1, d), fixed2),
                  pl.BlockSpec((d, IN_COLS), fixed2),
                  pl.BlockSpec((1, ATTN_WIDTH), fixed2),
                  pl.BlockSpec((1, 256), fixed2),
                  pl.BlockSpec((tm, LANES), lambda bb, i: (i, 0)),
                  pl.BlockSpec((tm, LANES), lambda bb, i: (i, 0)),
                  pl.BlockSpec((ATTN_WIDTH, ATTN_WIDTH), fixed2),
                  pl.BlockSpec((1, SGU_WIDTH), fixed2),
                  pl.BlockSpec((SGU_GROUPS, SGU_CHUNK, SGU_CHUNK), lambda bb, i: (0, 0, 0)),
                  pl.BlockSpec((SGU_CHUNK, SGU_WIDTH), fixed2)],
        out_specs=(pl.BlockSpec((1, tm, ATTN_WIDTH), row),
                   pl.BlockSpec((1, tm, 256), row),
                   pl.BlockSpec((1, tm, 256), row),
                   pl.BlockSpec((1, tm, S5_WIDTH), row),
                   pl.BlockSpec((1, tm, SGU_WIDTH), row)),
        scratch_shapes=[pltpu.VMEM((d, IN_COLS), BF16)],
        compiler_params=_cparams(2),
        name="inproj_rope" if rope else "inproj_ctx",
    )(h, shift, scale, g, w_ext, gq, gk, cos_t, sin_t, ones_blk, sgun, sguw, sgub)


def _stack_group(qj, kh, lane_half):
    pieces = []
    for g in range(KV_GROUP):
        head = kh * KV_GROUP + g
        qc = qj[:, (head // 2) * LANES:(head // 2 + 1) * LANES]
        pieces.append(jnp.where(lane_half == (head % 2), qc, jnp.zeros_like(qc)))
    return jnp.concatenate(pieces, axis=0)


def _sink_col(sink_ref, kh):
    rows = lax.broadcasted_iota(jnp.int32, (KV_GROUP * WINDOW, 1), 0) // WINDOW
    col = jnp.full((KV_GROUP * WINDOW, 1), sink_ref[kh * KV_GROUP], F32)
    for g in range(1, KV_GROUP):
        col = jnp.where(rows == g, sink_ref[kh * KV_GROUP + g], col)
    return col


def _unstack_store(o_ref, o, j, kh, lane_half_f):
    lo = lane_half_f == 0
    r = slice(j * WINDOW, (j + 1) * WINDOW)
    o_ref[0, r, (2 * kh) * LANES:(2 * kh + 1) * LANES] = jnp.where(lo, o[0:128], o[128:256])
    o_ref[0, r, (2 * kh + 1) * LANES:(2 * kh + 2) * LANES] = jnp.where(lo, o[256:384], o[384:512])


def _attn_win_kernel(sink_ref, q_ref, kc_ref, vc_ref, kp_ref, vp_ref, kn_ref, vn_ref, kx_ref, vx_ref, o_ref):
    i = pl.program_id(1)
    nt = pl.num_programs(1)
    tq = q_ref.shape[1]
    nsub = tq // WINDOW
    kfull = jnp.concatenate([kp_ref[0], kc_ref[0], kn_ref[0]], axis=0)
    vfull = jnp.concatenate([vp_ref[0], vc_ref[0], vn_ref[0]], axis=0)
    rows = KV_GROUP * WINDOW
    ri = lax.broadcasted_iota(jnp.int32, (rows, 3 * WINDOW), 0) % WINDOW
    ci = lax.broadcasted_iota(jnp.int32, (rows, 3 * WINDOW), 1)
    cblk = ci // WINDOW
    cj = ci % WINDOW
    ninf = np.float32(-np.inf)
    band = (cblk == 1) | ((cblk == 0) & (cj >= ri)) | ((cblk == 2) & (cj <= ri))
    bias0 = jnp.where(band, 0.0, ninf).astype(F32)
    no_prev = jnp.where(i > 0, 0.0, ninf).astype(F32)
    no_next = jnp.where(i < nt - 1, 0.0, ninf).astype(F32)
    lane_half = lax.broadcasted_iota(jnp.int32, (WINDOW, LANES), 1) // HEAD_DIM
    for j in range(nsub):
        qj = q_ref[0, j * WINDOW:(j + 1) * WINDOW, :]
        bias = bias0
        if j == 0:
            bias = bias + jnp.where(cblk == 0, no_prev, 0.0)
        if j == nsub - 1:
            bias = bias + jnp.where(cblk == 2, no_next, 0.0)
        for kh in range(N_KV_HEADS):
            qs = _stack_group(qj, kh, lane_half)
            ks = slice(kh * LANES, (kh + 1) * LANES)
            kw = kfull[j * WINDOW:(j + 3) * WINDOW, ks]
            vw = vfull[j * WINDOW:(j + 3) * WINDOW, ks]
            s_w = _dot_nt(qs, kw) + bias
            s_c = _dot_nt(qs, kx_ref[0, :, ks])
            sink = _sink_col(sink_ref, kh)
            m = jnp.maximum(jnp.maximum(s_w.max(-1, keepdims=True), s_c.max(-1, keepdims=True)), sink)
            e_w = jnp.exp(s_w - m)
            e_c = jnp.exp(s_c - m)
            den = e_w.sum(-1, keepdims=True) + e_c.sum(-1, keepdims=True) + jnp.exp(sink - m)
            o = _dot(e_w.astype(BF16), vw) + _dot(e_c.astype(BF16), vx_ref[0, :, ks])
            _unstack_store(o_ref, o / den, j, kh, lane_half)


def _attn_ctx_kernel(sink_ref, q_ref, kx_ref, vx_ref, o_ref):
    tq = q_ref.shape[1]
    lane_half = lax.broadcasted_iota(jnp.int32, (WINDOW, LANES), 1) // HEAD_DIM
    for j in range(tq // WINDOW):
        qj = q_ref[0, j * WINDOW:(j + 1) * WINDOW, :]
        for kh in range(N_KV_HEADS):
            qs = _stack_group(qj, kh, lane_half)
            ks = slice(kh * LANES, (kh + 1) * LANES)
            s_c = _dot_nt(qs, kx_ref[0, :, ks])
            sink = _sink_col(sink_ref, kh)
            m = jnp.maximum(s_c.max(-1, keepdims=True), sink)
            e_c = jnp.exp(s_c - m)
            den = e_c.sum(-1, keepdims=True) + jnp.exp(sink - m)
            o = _dot(e_c.astype(BF16), vx_ref[0, :, ks])
            _unstack_store(o_ref, o / den, j, kh, lane_half)


def _attn_win_call(sink, q, k, v, kx, vx):
    b, n, _ = q.shape
    lc = kx.shape[1]
    tq = min(ROW_TILE, n)
    nsub = tq // WINDOW
    nblk = n // WINDOW
    assert n % tq == 0 and tq % WINDOW == 0
    row = lambda bb, i: (bb, i, 0)
    prev = lambda bb, i: (bb, jnp.maximum(i * nsub - 1, 0), 0)
    nxt = lambda bb, i: (bb, jnp.minimum((i + 1) * nsub, nblk - 1), 0)
    ctx = lambda bb, i: (bb, 0, 0)
    return pl.pallas_call(
        _attn_win_kernel,
        out_shape=jax.ShapeDtypeStruct((b, n, ATTN_WIDTH), F32),
        grid=(b, n // tq),
        in_specs=[pl.BlockSpec(memory_space=pltpu.SMEM),
                  pl.BlockSpec((1, tq, ATTN_WIDTH), row),
                  pl.BlockSpec((1, tq, 256), row), pl.BlockSpec((1, tq, 256), row),
                  pl.BlockSpec((1, WINDOW, 256), prev), pl.BlockSpec((1, WINDOW, 256), prev),
                  pl.BlockSpec((1, WINDOW, 256), nxt), pl.BlockSpec((1, WINDOW, 256), nxt),
                  pl.BlockSpec((1, lc, 256), ctx), pl.BlockSpec((1, lc, 256), ctx)],
        out_specs=pl.BlockSpec((1, tq, ATTN_WIDTH), row),
        compiler_params=_cparams(2),
        name="attn_window",
    )(sink, q, k, v, k, v, k, v, kx, vx)


def _attn_ctx_call(sink, q, kx, vx):
    b, lc, _ = q.shape
    assert lc % WINDOW == 0
    blk = lambda bb: (bb, 0, 0)
    return pl.pallas_call(
        _attn_ctx_kernel,
        out_shape=jax.ShapeDtypeStruct((b, lc, ATTN_WIDTH), F32),
        grid=(b,),
        in_specs=[pl.BlockSpec(memory_space=pltpu.SMEM),
                  pl.BlockSpec((1, lc, ATTN_WIDTH), blk),
                  pl.BlockSpec((1, lc, 256), blk), pl.BlockSpec((1, lc, 256), blk)],
        out_specs=pl.BlockSpec((1, lc, ATTN_WIDTH), blk),
        compiler_params=_cparams(1),
        name="attn_ctx",
    )(sink, q, kx, vx)


def _s5_kernel(u_ref, perm_ref, permt_ref, bre_ref, bim_ref, a_ref, al_ref, cre_ref, cim_ref, h0_ref,
               y_ref, hfin_ref, bure_ref, buim_ref, hsre_ref, hsim_ref, carry_ref):
    t = u_ref.shape[1]
    seg_len = t // S5_SEGMENTS

    @pl.when(pl.program_id(1) == 0)
    def _():
        carry_ref[...] = h0_ref[0]

    up = _dot(perm_ref[...], u_ref[0].astype(BF16)).astype(BF16)
    bure_ref[...] = _dot(up, bre_ref[...])
    buim_ref[...] = _dot(up, bim_ref[...])
    a_re = jnp.broadcast_to(a_ref[0:1, :], (S5_SEGMENTS, S5_NSTATE))
    a_im = jnp.broadcast_to(a_ref[1:2, :], (S5_SEGMENTS, S5_NSTATE))

    def step(i, h):
        hr, hi = h
        r0 = pl.multiple_of(i * S5_SEGMENTS, S5_SEGMENTS)
        nr = a_re * hr - a_im * hi + bure_ref[pl.ds(r0, S5_SEGMENTS), :]
        ni = a_re * hi + a_im * hr + buim_ref[pl.ds(r0, S5_SEGMENTS), :]
        return nr, ni

    zero = jnp.zeros((S5_SEGMENTS, S5_NSTATE), F32)
    end_re, end_im = lax.fori_loop(0, seg_len, step, (zero, zero))

    al_re, al_im = al_ref[0:1, :], al_ref[1:2, :]
    c_re, c_im = carry_ref[0:1, :], carry_ref[1:2, :]
    in_re, in_im = [], []
    for s in range(S5_SEGMENTS):
        in_re.append(c_re)
        in_im.append(c_im)
        c_re, c_im = (al_re * c_re - al_im * c_im + end_re[s:s + 1, :],
                      al_re * c_im + al_im * c_re + end_im[s:s + 1, :])
    carry_ref[0:1, :] = c_re
    carry_ref[1:2, :] = c_im
    hfin_ref[0, 0:1, :] = c_re
    hfin_ref[0, 1:2, :] = c_im

    def step_store(i, h):
        nr, ni = step(i, h)
        r0 = pl.multiple_of(i * S5_SEGMENTS, S5_SEGMENTS)
        hsre_ref[pl.ds(r0, S5_SEGMENTS), :] = nr
        hsim_ref[pl.ds(r0, S5_SEGMENTS), :] = ni
        return nr, ni

    lax.fori_loop(0, seg_len, step_store, (jnp.concatenate(in_re, axis=0), jnp.concatenate(in_im, axis=0)))

    yp = _dot(hsre_ref[...].astype(BF16), cre_ref[...]) - _dot(hsim_ref[...].astype(BF16), cim_ref[...])
    yh, yl = _split(yp)
    y_ref[0] = _dot(permt_ref[...], yh) + _dot(permt_ref[...], yl)


def _s5_call(u, perm, permt, bre, bim, a, al, cre, cim, h0, *, reverse):
    b, n, w = u.shape
    t = perm.shape[0]
    nc = n // t
    assert n % t == 0
    if reverse:
        row = lambda bb, j: (bb, nc - 1 - j, 0)
    else:
        row = lambda bb, j: (bb, j, 0)
    fixed = lambda bb, j: (0, 0)
    return pl.pallas_call(
        _s5_kernel,
        out_shape=(jax.ShapeDtypeStruct((b, n, w), F32), jax.ShapeDtypeStruct((b, 2, S5_NSTATE), F32)),
        grid=(b, nc),
        in_specs=[pl.BlockSpec((1, t, w), row),
                  pl.BlockSpec((t, t), fixed), pl.BlockSpec((t, t), fixed),
                  pl.BlockSpec((w, S5_NSTATE), fixed), pl.BlockSpec((w, S5_NSTATE), fixed),
                  pl.BlockSpec((2, S5_NSTATE), fixed), pl.BlockSpec((2, S5_NSTATE), fixed),
                  pl.BlockSpec((S5_NSTATE, w), fixed), pl.BlockSpec((S5_NSTATE, w), fixed),
                  pl.BlockSpec((1, 2, S5_NSTATE), lambda bb, j: (bb, 0, 0))],
        out_specs=(pl.BlockSpec((1, t, w), row),
                   pl.BlockSpec((1, 2, S5_NSTATE), lambda bb, j: (bb, 0, 0))),
        scratch_shapes=[pltpu.VMEM((t, S5_NSTATE), F32)] * 4 + [pltpu.VMEM((2, S5_NSTATE), F32)],
        compiler_params=_cparams(2),
        name="s5_scan_rev" if reverse else "s5_scan_fwd",
    )(u, perm, permt, bre, bim, a, al, cre, cim, h0)


def _s5_perm(t, reverse):
    seg_len = t // S5_SEGMENTS
    r = np.arange(t)
    src = (r % S5_SEGMENTS) * seg_len + r // S5_SEGMENTS
    if reverse:
        src = t - 1 - src
    p = np.zeros((t, t), np.float32)
    p[r, src] = 1.0
    return jnp.asarray(p, BF16), jnp.asarray(p.T, BF16)


def _s5_params(lam_re, lam_im, log_dt, b_re, b_im, c_re, c_im, seg_lens):
    lead = lam_re.shape[:-2]
    dt = jnp.exp(log_dt.astype(F32))[..., None]
    lr, li = lam_re.astype(F32), lam_im.astype(F32)
    mag = jnp.exp(lr * dt)
    ang = li * dt
    ab_re, ab_im = mag * jnp.cos(ang), mag * jnp.sin(ang)
    nr, ni = ab_re - 1.0, ab_im
    den = lr * lr + li * li
    coef_re = (nr * lr + ni * li) / den
    coef_im = (ni * lr - nr * li) / den
    br, bi = b_re.astype(F32), b_im.astype(F32)
    bb_re = coef_re[..., None] * br - coef_im[..., None] * bi
    bb_im = coef_re[..., None] * bi + coef_im[..., None] * br
    eye = jnp.eye(S5_GROUPS, dtype=F32)

    def in_mat(bb):
        return jnp.einsum("...gpc,gh->...gchp", bb, eye).reshape(*lead, S5_WIDTH, S5_NSTATE).astype(BF16)

    def out_mat(cc):
        return jnp.einsum("...gcp,gh->...gphc", cc.astype(F32), eye).reshape(*lead, S5_NSTATE, S5_WIDTH).astype(BF16)

    flat = lambda v: v.reshape(*lead, S5_NSTATE)
    a = jnp.stack([flat(ab_re), flat(ab_im)], axis=-2)
    powers, pr, pi, length = {}, ab_re, ab_im, 1
    while length < max(seg_lens):
        pr, pi = pr * pr - pi * pi, 2.0 * pr * pi
        length *= 2
        if length in seg_lens:
            powers[length] = jnp.stack([flat(pr), flat(pi)], axis=-2)
    assert set(powers) == set(seg_lens)
    return in_mat(bb_re), in_mat(bb_im), a, powers, out_mat(c_re), out_mat(c_im)


def _mix_kernel(h_ref, att_ref, yf_ref, yb_ref, us_ref, sg_ref, gate_ref, shift_ref, scale_ref,
                dskip_ref, gluw_ref, glub_ref, outg_ref, wout_ref, gffn_ref, rw_ref,
                hnew_ref, f_ref, p_ref, w16_ref):
    @pl.when((pl.program_id(0) == 0) & (pl.program_id(1) == 0))
    def _():
        w16_ref[...] = wout_ref[0].astype(BF16)

    y = yf_ref[0] + yb_ref[0] + dskip_ref[...] * us_ref[0]
    g = _gelu_tanh(y)
    s5 = g * _sigmoid(_dot(g.astype(BF16), gluw_ref[...]) + glub_ref[...])
    outg = outg_ref[...]
    o1, o2 = ATTN_WIDTH, ATTN_WIDTH + S5_WIDTH
    pa = _rms(att_ref[0], outg[:, :o1]).astype(BF16)
    ps = _rms(s5, outg[:, o1:o2]).astype(BF16)
    pg = _rms(sg_ref[0], outg[:, o2:]).astype(BF16)
    mixed = _dot(pa, w16_ref[:o1, :]) + _dot(ps, w16_ref[o1:o2, :]) + _dot(pg, w16_ref[o2:, :])
    hn = h_ref[0] + gate_ref[0] * mixed
    hnew_ref[0] = hn

    f = _rms(hn, gffn_ref[...]) * (1.0 + scale_ref[0]) + shift_ref[0]
    fh, fl = _split(f)
    f_ref[0] = fh
    rh, rl = _split(rw_ref[...])
    logits = _dot_nt(rh, fh) + _dot_nt(rh, fl) + _dot_nt(rl, fh)
    m = logits.max(axis=0, keepdims=True)
    e = jnp.exp(logits - m)
    p_ref[0] = e / e.sum(axis=0, keepdims=True)


def _mix_call(layer, h, att, yf, yb, us, sg, gate, shift, scale, dskip, gluw, glub, outg, wout, gffn, rwt):
    b, n, d = h.shape
    tm = min(ROW_TILE, n)
    assert n % tm == 0
    row = lambda bb, i: (bb, i, 0)
    vec = lambda bb, i: (bb, 0, 0)
    fixed = lambda bb, i: (0, 0)
    return pl.pallas_call(
        _mix_kernel,
        out_shape=(jax.ShapeDtypeStruct((b, n, d), F32),
                   jax.ShapeDtypeStruct((b, n, d), BF16),
                   jax.ShapeDtypeStruct((b, N_EXPERTS, n), F32)),
        grid=(b, n // tm),
        in_specs=[pl.BlockSpec((1, tm, d), row),
                  pl.BlockSpec((1, tm, ATTN_WIDTH), row),
                  pl.BlockSpec((1, tm, S5_WIDTH), row), pl.BlockSpec((1, tm, S5_WIDTH), row),
                  pl.BlockSpec((1, tm, S5_WIDTH), row), pl.BlockSpec((1, tm, SGU_WIDTH), row),
                  pl.BlockSpec((1, 1, d), vec), pl.BlockSpec((1, 1, d), vec), pl.BlockSpec((1, 1, d), vec),
                  pl.BlockSpec((1, S5_WIDTH), fixed),
                  pl.BlockSpec((S5_WIDTH, S5_WIDTH), fixed),
                  pl.BlockSpec((1, S5_WIDTH), fixed),
                  pl.BlockSpec((1, d), fixed),
                  pl.BlockSpec((1, d, d), lambda bb, i: (layer, 0, 0)),
                  pl.BlockSpec((1, d), fixed),
                  pl.BlockSpec((N_EXPERTS, d), fixed)],
        out_specs=(pl.BlockSpec((1, tm, d), row), pl.BlockSpec((1, tm, d), row),
                   pl.BlockSpec((1, N_EXPERTS, tm), lambda bb, i: (bb, 0, i))),
        scratch_shapes=[pltpu.VMEM((d, d), BF16)],
        compiler_params=_cparams(2),
        name="mix_out_router",
    )(h, att, yf, yb, us, sg, gate, shift, scale, dskip, gluw, glub, outg, wout, gffn, rwt)


def _thr_kernel(p_ref, thr_ref, allow_ref, *, cap):
    bits = pltpu.bitcast(p_ref[0], jnp.int32)

    def count(mask):
        return jnp.sum(mask.astype(F32), axis=1, keepdims=True)

    def body(it, thr):
        cand = thr | jnp.left_shift(jnp.int32(1), 30 - it)
        return jnp.where(count(bits >= cand) >= cap, cand, thr)

    thr = lax.fori_loop(0, 31, body, jnp.zeros((N_EXPERTS, 1), jnp.int32))
    allow = (cap - count(bits > thr)).astype(jnp.int32)
    thr_ref[0] = jnp.broadcast_to(thr, (N_EXPERTS, LANES))
    allow_ref[0] = jnp.broadcast_to(allow, (N_EXPERTS, LANES))


def _thr_call(probs, cap):
    b, e, n = probs.shape
    blk = lambda bb: (bb, 0, 0)
    return pl.pallas_call(
        functools.partial(_thr_kernel, cap=cap),
        out_shape=(jax.ShapeDtypeStruct((b, e, LANES), jnp.int32),) * 2,
        grid=(b,),
        in_specs=[pl.BlockSpec((1, e, n), blk)],
        out_specs=(pl.BlockSpec((1, e, LANES), blk),) * 2,
        compiler_params=_cparams(1),
        name="route_threshold",
    )(probs)


def _rank_kernel(p_ref, thr_ref, allow_ref, rank_ref, offs_ref, eqseen_ref, selseen_ref):
    @pl.when(pl.program_id(1) == 0)
    def _():
        eqseen_ref[...] = jnp.zeros_like(eqseen_ref)
        selseen_ref[...] = jnp.zeros_like(selseen_ref)

    tt = p_ref.shape[2]
    bits = pltpu.bitcast(p_ref[0], jnp.int32)
    thr = thr_ref[0][:, 0:1]
    allow = allow_ref[0][:, 0:1].astype(F32)
    gt = bits > thr
    eq = bits == thr
    before = (lax.broadcasted_iota(jnp.int32, (tt, tt), 0) <
              lax.broadcasted_iota(jnp.int32, (tt, tt), 1)).astype(F32).astype(BF16)
    eq_seen = eqseen_ref[...][:, 0:1]
    eqf = eq.astype(F32)
    eq_before = eq_seen + _dot(eqf.astype(BF16), before)
    sel = gt | (eq & (eq_before < allow))
    self32 = sel.astype(F32)
    sel_seen = selseen_ref[...][:, 0:1]
    rank = sel_seen + _dot(self32.astype(BF16), before)
    rank_ref[0] = jnp.where(sel, rank, -1.0)
    offs_ref[0, 0] = jnp.broadcast_to(sel_seen, (N_EXPERTS, LANES)).astype(jnp.int32)
    eqseen_ref[...] = jnp.broadcast_to(eq_seen + eqf.sum(axis=1, keepdims=True), (N_EXPERTS, LANES))
    selseen_ref[...] = jnp.broadcast_to(sel_seen + self32.sum(axis=1, keepdims=True), (N_EXPERTS, LANES))


def _rank_call(probs, thr, allow, tt):
    b, e, n = probs.shape
    nt = n // tt
    blk = lambda bb, t: (bb, 0, 0)
    return pl.pallas_call(
        _rank_kernel,
        out_shape=(jax.ShapeDtypeStruct((b, e, n), F32),
                   jax.ShapeDtypeStruct((b, nt, e, LANES), jnp.int32)),
        grid=(b, nt),
        in_specs=[pl.BlockSpec((1, e, tt), lambda bb, t: (bb, 0, t)),
                  pl.BlockSpec((1, e, LANES), blk), pl.BlockSpec((1, e, LANES), blk)],
        out_specs=(pl.BlockSpec((1, e, tt), lambda bb, t: (bb, 0, t)),
                   pl.BlockSpec((1, 1, e, LANES), lambda bb, t: (bb, t, 0, 0))),
        scratch_shapes=[pltpu.VMEM((e, LANES), F32)] * 2,
        compiler_params=_cparams(2),
        name="route_rank",
    )(probs, thr, allow)


def _window_rounds(offs_sm, row, base, win):
    o0, o1 = offs_sm[row], offs_sm[row + 1]
    return jnp.where(o1 > o0, (o1 - base + win - 1) // win, 0)


def _window_hit(rank_row, lo, start, iota):
    rel = jnp.where(rank_row >= lo.astype(F32), rank_row, -1.0) - start.astype(F32)
    return rel == iota


def _gather_kernel(offs_sm, rank_ref, p_ref, x_ref, xin_ref, gsel_ref, *, win):
    bb, eg, t = pl.program_id(0), pl.program_id(1), pl.program_id(2)
    n_t = pl.num_programs(2)
    grp, cap = xin_ref.shape[1], xin_ref.shape[2]
    tt = x_ref.shape[1]

    @pl.when(t == 0)
    def _():
        xin_ref[...] = jnp.zeros_like(xin_ref)
        gsel_ref[...] = jnp.zeros_like(gsel_ref)

    base, rounds = [], jnp.int32(0)
    for g in range(grp):
        row = ((bb * pl.num_programs(1) + eg) * grp + g) * (n_t + 1) + t
        base.append((offs_sm[row] // ROW_ALIGN) * ROW_ALIGN)
        rounds = jnp.maximum(rounds, _window_rounds(offs_sm, row, base[g], win))
    iota = lax.broadcasted_iota(jnp.int32, (win, tt), 0).astype(F32)

    def one_round(r, carry):
        hits, starts = [], []
        for g in range(grp):
            lo = base[g] + r * win
            starts.append(pl.multiple_of(jnp.minimum(lo, cap - win), ROW_ALIGN))
            hits.append(_window_hit(rank_ref[0, 0, g:g + 1, :], lo, starts[g], iota))
        onehot = jnp.concatenate([h.astype(F32).astype(BF16) for h in hits], axis=0)
        res = _dot(onehot, x_ref[0])
        for g in range(grp):
            rows = pl.ds(starts[g], win)
            xin_ref[0, g, rows, :] += res[g * win:(g + 1) * win].astype(BF16)
            gate = jnp.sum(jnp.where(hits[g], p_ref[0, 0, g:g + 1, :], 0.0), axis=1, keepdims=True)
            gsel_ref[0, g, rows, :] += jnp.broadcast_to(gate, (win, LANES))
        return carry

    lax.fori_loop(0, rounds, one_round, 0)


def _gather_call(offs1, rank, probs, f16, cap, win, tt):
    b, n, d = f16.shape
    e, grp = N_EXPERTS, MOE_GROUP
    ng, nt = e // grp, n // tt
    grid_spec = pltpu.PrefetchScalarGridSpec(
        num_scalar_prefetch=1,
        grid=(b, ng, nt),
        in_specs=[pl.BlockSpec((1, 1, grp, tt), lambda bb, gg, t, o: (bb, gg, 0, t)),
                  pl.BlockSpec((1, 1, grp, tt), lambda bb, gg, t, o: (bb, gg, 0, t)),
                  pl.BlockSpec((1, tt, d), lambda bb, gg, t, o: (bb, t, 0))],
        out_specs=(pl.BlockSpec((1, grp, cap, d), lambda bb, gg, t, o: (bb, gg, 0, 0)),
                   pl.BlockSpec((1, grp, cap, LANES), lambda bb, gg, t, o: (bb, gg, 0, 0))),
    )
    return pl.pallas_call(
        functools.partial(_gather_kernel, win=win),
        out_shape=(jax.ShapeDtypeStruct((b, e, cap, d), BF16),
                   jax.ShapeDtypeStruct((b, e, cap, LANES), F32)),
        grid_spec=grid_spec,
        compiler_params=_cparams(3),
        name="moe_gather",
    )(offs1, rank.reshape(b, ng, grp, n), probs.reshape(b, ng, grp, n), f16)


def _ffn_kernel(*refs, has_ctx):
    if has_ctx:
        x_ref, g_ref, xc_ref, gc_ref, wg_ref, wu_ref, wd_ref, y_ref, yc_ref, wg16, wu16, wd16 = refs
    else:
        x_ref, g_ref, wg_ref, wu_ref, wd_ref, y_ref, wg16, wu16, wd16 = refs
    m = pl.program_id(2)

    @pl.when((pl.program_id(1) == 0) & (m == 0))
    def _():
        wg16[...] = wg_ref[0, 0].astype(BF16)
        wu16[...] = wu_ref[0, 0].astype(BF16)
        wd16[...] = wd_ref[0, 0].astype(BF16)

    def ffn(x, gate):
        hg = _dot(x, wg16[...])
        hid = (hg * _sigmoid(hg)) * _dot(x, wu16[...])
        return (_dot(hid.astype(BF16), wd16[...]) * gate).astype(BF16)

    if not has_ctx:
        y_ref[0, 0] = ffn(x_ref[0, 0], g_ref[0, 0][:, 0:1])
        return
    tm = x_ref.shape[2]

    @pl.when(m == 0)
    def _():
        x = jnp.concatenate([x_ref[0, 0], xc_ref[0, 0]], axis=0)
        gate = jnp.concatenate([g_ref[0, 0][:, 0:1], gc_ref[0, 0][:, 0:1]], axis=0)
        y = ffn(x, gate)
        y_ref[0, 0] = y[:tm]
        yc_ref[0, 0] = y[tm:]

    @pl.when(m > 0)
    def _():
        y_ref[0, 0] = ffn(x_ref[0, 0], g_ref[0, 0][:, 0:1])


def _ffn_call(layer, xin, gsel, wg, wu, wd, ctx_rows=None):
    b, e, cap, d = xin.shape
    ff = wg.shape[3]
    tm = min(ROW_TILE, cap)
    assert cap % tm == 0
    xrow = lambda ee, bb, m: (bb, ee, m, 0)
    xctx = lambda ee, bb, m: (bb, ee, 0, 0)
    wblk = lambda ee, bb, m: (layer, ee, 0, 0)
    in_specs = [pl.BlockSpec((1, 1, tm, d), xrow), pl.BlockSpec((1, 1, tm, LANES), xrow)]
    out_shape = [jax.ShapeDtypeStruct((b, e, cap, d), BF16)]
    out_specs = [pl.BlockSpec((1, 1, tm, d), xrow)]
    args = [xin, gsel]
    if ctx_rows is not None:
        xc, gc = ctx_rows
        capc = xc.shape[2]
        in_specs += [pl.BlockSpec((1, 1, capc, d), xctx), pl.BlockSpec((1, 1, capc, LANES), xctx)]
        out_shape.append(jax.ShapeDtypeStruct((b, e, capc, d), BF16))
        out_specs.append(pl.BlockSpec((1, 1, capc, d), xctx))
        args += [xc, gc]
    in_specs += [pl.BlockSpec((1, 1, d, ff), wblk), pl.BlockSpec((1, 1, d, ff), wblk),
                 pl.BlockSpec((1, 1, ff, d), wblk)]
    return pl.pallas_call(
        functools.partial(_ffn_kernel, has_ctx=ctx_rows is not None),
        out_shape=tuple(out_shape),
        grid=(e, b, cap // tm),
        in_specs=in_specs,
        out_specs=tuple(out_specs),
        scratch_shapes=[pltpu.VMEM((d, ff), BF16), pltpu.VMEM((d, ff), BF16), pltpu.VMEM((ff, d), BF16)],
        compiler_params=_cparams(3),
        name="moe_expert_ffn",
    )(*args, wg, wu, wd)


def _combine_kernel(offs_sm, rank_ref, y_hbm, h_ref, gate_ref, o_ref, win_ref, sem, *, win):
    bb, t = pl.program_id(0), pl.program_id(1)
    n_b, n_t = pl.num_programs(0), pl.num_programs(1)
    step = bb * n_t + t
    slot = step % 2
    n_e = rank_ref.shape[1]
    cap = y_hbm.shape[2]
    tt = h_ref.shape[1]

    def offs_row(b_i, e, t_i):
        return (b_i * n_e + e) * (n_t + 1) + t_i

    def bases(b_i, t_i):
        return [(offs_sm[offs_row(b_i, e, t_i)] // ROW_ALIGN) * ROW_ALIGN for e in range(n_e)]

    def window_start(lo):
        return pl.multiple_of(jnp.minimum(lo, cap - win), ROW_ALIGN)

    def copies(b_i, base, r, slot_i):
        return [pltpu.make_async_copy(y_hbm.at[b_i, e, pl.ds(window_start(base[e] + r * win), win), :],
                                      win_ref.at[slot_i, pl.ds(e * win, win), :],
                                      sem.at[slot_i, e])
                for e in range(n_e)]

    @pl.when(step == 0)
    def _():
        for cp in copies(bb, bases(bb, t), 0, slot):
            cp.start()

    @pl.when(step + 1 < n_b * n_t)
    def _():
        wrap = t + 1 == n_t
        b2 = jnp.where(wrap, bb + 1, bb)
        t2 = jnp.where(wrap, 0, t + 1)
        for cp in copies(b2, bases(b2, t2), 0, 1 - slot):
            cp.start()

    base = bases(bb, t)
    rounds = jnp.int32(0)
    for e in range(n_e):
        rounds = jnp.maximum(rounds, _window_rounds(offs_sm, offs_row(bb, e, t), base[e], win))
    iota = lax.broadcasted_iota(jnp.int32, (win, tt), 0).astype(F32)

    def contribution(r):
        hits = []
        for e in range(n_e):
            lo = base[e] + r * win
            hit = _window_hit(rank_ref[0, e:e + 1, :], lo, window_start(lo), iota)
            hits.append(hit.astype(F32).astype(BF16))
        return _dot_tn(jnp.concatenate(hits, axis=0), win_ref[slot])

    for cp in copies(bb, base, 0, slot):
        cp.wait()
    o_ref[0] = h_ref[0] + gate_ref[0] * contribution(0)

    def extra_round(r, carry):
        cps = copies(bb, base, r, slot)
        for cp in cps:
            cp.start()
        for cp in cps:
            cp.wait()
        o_ref[0] += gate_ref[0] * contribution(r)
        return carry

    lax.fori_loop(1, rounds, extra_round, 0)


def _combine_call(offs1, rank, y, h, gate, win, tt):
    b, n, d = h.shape
    e = N_EXPERTS
    grid_spec = pltpu.PrefetchScalarGridSpec(
        num_scalar_prefetch=1,
        grid=(b, n // tt),
        in_specs=[pl.BlockSpec((1, e, tt), lambda bb, t, o: (bb, 0, t)),
                  pl.BlockSpec(memory_space=pl.ANY),
                  pl.BlockSpec((1, tt, d), lambda bb, t, o: (bb, t, 0)),
                  pl.BlockSpec((1, 1, d), lambda bb, t, o: (bb, 0, 0))],
        out_specs=pl.BlockSpec((1, tt, d), lambda bb, t, o: (bb, t, 0)),
        scratch_shapes=[pltpu.VMEM((2, e * win, d), BF16), pltpu.SemaphoreType.DMA((2, e))],
    )
    return pl.pallas_call(
        functools.partial(_combine_kernel, win=win),
        out_shape=jax.ShapeDtypeStruct((b, n, d), F32),
        grid_spec=grid_spec,
        compiler_params=_cparams(2),
        name="moe_combine",
    )(offs1, rank, y, h, gate)


def _route(probs):
    b, e, n = probs.shape
    cap = EC_CAPACITY_FACTOR * n // e
    tt = min(ROW_TILE, n)
    thr, allow = _thr_call(probs, cap)
    rank, offs = _rank_call(probs, thr, allow, tt)
    offs = jnp.transpose(offs[..., 0], (0, 2, 1))
    offs1 = jnp.concatenate([offs, jnp.full((b, e, 1), cap, jnp.int32)], axis=-1)
    return rank, offs1.reshape(-1), cap, min(ROUTE_BLOCK, cap), tt


def _rope_tables(n):
    rows = n // GRID_W
    row = jnp.repeat(jnp.arange(rows), GRID_W).astype(F32)
    col = jnp.tile(jnp.arange(GRID_W), rows).astype(F32)
    axis_dim = HEAD_DIM // 2
    inv = ROPE_BASE ** (-jnp.arange(0, axis_dim, 2, dtype=F32) / axis_dim)
    ang_r = row[:, None] * inv
    ang_c = col[:, None] * inv
    cr, sr, cc, sc = jnp.cos(ang_r), jnp.sin(ang_r), jnp.cos(ang_c), jnp.sin(ang_c)
    cos64 = jnp.concatenate([cr, cr, cc, cc], axis=1)
    sin64 = jnp.concatenate([-sr, sr, -sc, sc], axis=1)
    return jnp.tile(cos64, (1, 2)), jnp.tile(sin64, (1, 2))


def _extend_w_in(w_in):
    q = w_in[:, :512]
    k = w_in[:, 512:640]
    v = w_in[:, 640:768]
    rest = w_in[:, 768:]
    dup = lambda m: jnp.concatenate([m[:, :64], m[:, :64], m[:, 64:], m[:, 64:]], axis=1)
    return jnp.concatenate([q, dup(k), dup(v), rest], axis=1)


def kernel(x, c, ctx, c_ctx, ada_w, ada_b, norm_mix_g, norm_ffn_g, w_in, q_norm_g, k_norm_g, attn_sink,
           s5_lambda_re, s5_lambda_im, s5_log_dt, s5_b_re, s5_b_im, s5_c_re, s5_c_im, s5_d, s5_glu_w,
           s5_glu_b, sgu_norm_g, sgu_w, sgu_b, out_norm_g, w_out, router_w, exp_w_gate, exp_w_up,
           exp_w_down):
    b, n, d = x.shape
    lc = ctx.shape[1]
    depth = ada_w.shape[0]
    assert b + 1 <= SUBLANES

    crows = jnp.zeros((SUBLANES, d), F32).at[:b].set(c).at[b].set(c_ctx)
    mod = _ada_call(crows, ada_w, ada_b)

    cos_t, sin_t = _rope_tables(n)
    cos_c, sin_c = cos_t[:lc], sin_t[:lc]
    ones_blk = jnp.asarray(np.arange(ATTN_WIDTH)[:, None] // HEAD_DIM == np.arange(ATTN_WIDTH)[None, :] // HEAD_DIM,
                           BF16)
    t_lat = min(ROW_TILE, n)
    t_ctx = min(ROW_TILE, lc)
    perms = {(t, r): _s5_perm(t, r) for t in {t_lat, t_ctx} for r in (False, True)}
    seg_lat, seg_ctx = t_lat // S5_SEGMENTS, t_ctx // S5_SEGMENTS
    s5_bre, s5_bim, s5_a, s5_apow, s5_cre, s5_cim = _s5_params(
        s5_lambda_re, s5_lambda_im, s5_log_dt, s5_b_re, s5_b_im, s5_c_re, s5_c_im, {seg_lat, seg_ctx})

    h_lat, h_ctx = x, ctx
    for l in range(depth):
        ctx_out = l < depth - 1
        mod_lat = mod[l, :b].reshape(b, N_MOD, 1, d)
        mod_ctx = jnp.broadcast_to(mod[l, b].reshape(1, N_MOD, 1, d), (b, N_MOD, 1, d))
        row = lambda v: v.reshape(1, -1)

        w_ext = _extend_w_in(w_in[l])
        gq = row(jnp.tile(q_norm_g[l], N_HEADS) * np.float32(HEAD_DIM ** -0.5))
        gk = row(jnp.tile(k_norm_g[l], 2 * N_KV_HEADS))
        sgub = jnp.repeat(sgu_b[l].T, SGU_WIDTH // SGU_GROUPS, axis=1)
        common = (row(norm_mix_g[l]), w_ext, gq, gk)
        sgu_args = (ones_blk, row(sgu_norm_g[l]), sgu_w[l], sgub)
        q_l, k_l, v_l, s_l, sg_l = _inproj_call(h_lat, mod_lat[:, 0], mod_lat[:, 1], *common, cos_t, sin_t,
                                                *sgu_args, rope=True)
        q_c, k_c, v_c, s_c, sg_c = _inproj_call(h_ctx, mod_ctx[:, 0], mod_ctx[:, 1], *common, cos_c, sin_c,
                                                *sgu_args, rope=False)

        att_lat = _attn_win_call(attn_sink[l], q_l, k_l, v_l, k_c, v_c)

        ys_lat, ys_ctx = [], []
        for direction in range(2):
            rev = direction == 1
            ld = (l, direction)
            bre, bim, a, cre, cim = s5_bre[ld], s5_bim[ld], s5_a[ld], s5_cre[ld], s5_cim[ld]
            h0 = jnp.zeros((b, 2, S5_NSTATE), F32)
            y_c, hfin = _s5_call(s_c, *perms[(t_ctx, rev)], bre, bim, a, s5_apow[seg_ctx][ld], cre, cim, h0,
                                 reverse=rev)
            y_l, _ = _s5_call(s_l, *perms[(t_lat, rev)], bre, bim, a, s5_apow[seg_lat][ld], cre, cim, hfin,
                              reverse=rev)
            ys_lat.append(y_l)
            ys_ctx.append(y_c)

        tail = (row(s5_d[l]), s5_glu_w[l].astype(BF16), row(s5_glu_b[l]), row(out_norm_g[l]), w_out,
                row(norm_ffn_g[l]), router_w[l].T)
        experts = (exp_w_gate, exp_w_up, exp_w_down)
        h_mid, f_lat, p_lat = _mix_call(l, h_lat, att_lat, ys_lat[0], ys_lat[1], s_l, sg_l,
                                        mod_lat[:, 2], mod_lat[:, 3], mod_lat[:, 4], *tail)
        rank_l, offs_l, cap_l, win_l, tt_l = _route(p_lat)
        xin_l, gsel_l = _gather_call(offs_l, rank_l, p_lat, f_lat, cap_l, win_l, tt_l)
        if ctx_out:
            att_ctx = _attn_ctx_call(attn_sink[l], q_c, k_c, v_c)
            hc_mid, f_ctx, p_ctx = _mix_call(l, h_ctx, att_ctx, ys_ctx[0], ys_ctx[1], s_c, sg_c,
                                             mod_ctx[:, 2], mod_ctx[:, 3], mod_ctx[:, 4], *tail)
            rank_c, offs_c, cap_c, win_c, tt_c = _route(p_ctx)
            xin_c, gsel_c = _gather_call(offs_c, rank_c, p_ctx, f_ctx, cap_c, win_c, tt_c)
            y_l, y_c = _ffn_call(l, xin_l, gsel_l, *experts, ctx_rows=(xin_c, gsel_c))
            h_ctx = _combine_call(offs_c, rank_c, y_c, hc_mid, mod_ctx[:, 5], win_c, tt_c)
        else:
            (y_l,) = _ffn_call(l, xin_l, gsel_l, *experts)
        h_lat = _combine_call(offs_l, rank_l, y_l, h_mid, mod_lat[:, 5], win_l, tt_l)
    return h_lat
```

```python
import functools

import numpy as np
import jax
import jax.numpy as jnp
from jax import lax
from jax.experimental import pallas as pl
from jax.experimental.pallas import tpu as pltpu

F32 = jnp.float32
BF16 = jnp.bfloat16

HEAD_DIM = 64
N_HEADS = 8
N_KV_HEADS = 2
KV_GROUP = N_HEADS // N_KV_HEADS
ATTN_WIDTH = N_HEADS * HEAD_DIM
WINDOW = 128
GRID_W = 64
ROPE_BASE = 10000.0
S5_WIDTH = 256
S5_GROUP = 16
S5_GROUPS = S5_WIDTH // S5_GROUP
S5_STATE = 64
S5_NSTATE = S5_GROUPS * S5_STATE
SGU_WIDTH = 256
SGU_GROUPS = 4
SGU_CHUNK = 128
N_EXPERTS = 16
EC_CAPACITY_FACTOR = 2
N_MOD = 6
EPS = 1e-6

LANES = 128
SUBLANES = 8
VMEM_LIMIT_BYTES = 56 * 1024 * 1024

ROW_TILE = 512
S5_SEGMENTS = SUBLANES
S5_UNROLL = 4
ROUTE_TILE = 512
ROUTE_BLOCK = 128
ROW_ALIGN = 16
MOE_GROUP = 4
GATHER_TILES = 4


def _cparams(n_axes):
    return pltpu.CompilerParams(dimension_semantics=("arbitrary",) * n_axes,
                                vmem_limit_bytes=VMEM_LIMIT_BYTES)


def _dot(a, b):
    return jnp.dot(a, b, preferred_element_type=F32)


def _dot_nt(a, b):
    return lax.dot_general(a, b, (((1,), (1,)), ((), ())), preferred_element_type=F32)


def _dot_tn(a, b):
    return lax.dot_general(a, b, (((0,), (0,)), ((), ())), preferred_element_type=F32)


def _split(x):
    hi = x.astype(BF16)
    lo = (x - hi.astype(F32)).astype(BF16)
    return hi, lo


def _sigmoid(x):
    return 1.0 / (1.0 + jnp.exp(-x))


def _gelu_tanh(x):
    c = np.float32(np.sqrt(2.0 / np.pi))
    return x * (0.5 * (1.0 + jnp.tanh(c * (x + np.float32(0.044715) * (x * x * x)))))


def _rms(x, g):
    return x * lax.rsqrt(jnp.mean(x * x, axis=-1, keepdims=True) + EPS) * g


def _ada_kernel(c_ref, w_ref, b_ref, o_ref):
    c = c_ref[...]
    sc = c * _sigmoid(c)
    ch, cl = _split(sc)
    wh, wl = _split(w_ref[0])
    o_ref[0] = _dot(ch, wh) + _dot(ch, wl) + _dot(cl, wh) + b_ref[0]


def _ada_call(crows, ada_w, ada_b):
    depth, d, nmod = ada_w.shape
    rows = crows.shape[0]
    tn = 1536
    assert nmod % tn == 0
    return pl.pallas_call(
        _ada_kernel,
        out_shape=jax.ShapeDtypeStruct((depth, rows, nmod), F32),
        grid=(depth, nmod // tn),
        in_specs=[pl.BlockSpec((rows, d), lambda l, j: (0, 0)),
                  pl.BlockSpec((1, d, tn), lambda l, j: (l, 0, j)),
                  pl.BlockSpec((1, 1, tn), lambda l, j: (l, 0, j))],
        out_specs=pl.BlockSpec((1, rows, tn), lambda l, j: (l, 0, j)),
        compiler_params=_cparams(2),
        name="ada_mod",
    )(crows, ada_w, ada_b.reshape(depth, 1, nmod))


IN_COLS = ATTN_WIDTH + 2 * (2 * N_KV_HEADS * HEAD_DIM) + S5_WIDTH + 2 * SGU_WIDTH
_QO, _KO, _VO, _SO, _GO = 0, 512, 768, 1024, 1280


def _head_sumsq(x, ones_blk):
    hi, lo = _split(x * x)
    return _dot(hi, ones_blk) + _dot(lo, ones_blk)


def _rope(x, cos, sin, lane_lo):
    outs = []
    for c in range(x.shape[1] // LANES):
        xc = x[:, c * LANES:(c + 1) * LANES]
        up = pltpu.roll(xc, LANES - 16, axis=1)
        dn = pltpu.roll(xc, 16, axis=1)
        partner = jnp.where(lane_lo, up, dn)
        outs.append(xc * cos + partner * sin)
    return jnp.concatenate(outs, axis=1)


def _inproj_kernel(h_ref, shift_ref, scale_ref, g_ref, w_ref, gq_ref, gk_ref, cos_ref, sin_ref,
                   ones_ref, sgun_ref, sguw_ref, sgub_ref,
                   q_ref, k_ref, v_ref, s_ref, sg_ref, w16_ref, *, rope):
    @pl.when((pl.program_id(0) == 0) & (pl.program_id(1) == 0))
    def _():
        w16_ref[...] = w_ref[...].astype(BF16)

    x = h_ref[0]
    a = _rms(x, g_ref[...]) * (1.0 + scale_ref[0]) + shift_ref[0]
    z = _dot(a.astype(BF16), w16_ref[...])
    tm = z.shape[0]

    q = z[:, _QO:_KO]
    k = z[:, _KO:_VO]
    ones_blk = ones_ref[...]
    qn = q * lax.rsqrt(_head_sumsq(q, ones_blk) * (1.0 / HEAD_DIM) + EPS) * gq_ref[...]
    kn = k * lax.rsqrt(_head_sumsq(k, ones_blk[:256, :256]) * (1.0 / HEAD_DIM) + EPS) * gk_ref[...]
    if rope:
        lane_lo = (lax.broadcasted_iota(jnp.int32, (tm, LANES), 1) % 32) < 16
        cos, sin = cos_ref[...], sin_ref[...]
        qn = _rope(qn, cos, sin, lane_lo)
        kn = _rope(kn, cos, sin, lane_lo)
    q_ref[0] = qn.astype(BF16)
    k_ref[0] = kn.astype(BF16)
    v_ref[0] = z[:, _VO:_SO].astype(BF16)
    s_ref[0] = z[:, _SO:_GO]

    gz = _gelu_tanh(z[:, _GO:])
    u = gz[:, :SGU_WIDTH]
    vv = _rms(gz[:, SGU_WIDTH:], sgun_ref[...]).astype(BF16)
    cg = SGU_WIDTH // SGU_GROUPS
    lane_grp = lax.broadcasted_iota(jnp.int32, (SGU_CHUNK, SGU_WIDTH), 1) // cg
    bias = sgub_ref[...]
    for ch in range(tm // SGU_CHUNK):
        vc = vv[ch * SGU_CHUNK:(ch + 1) * SGU_CHUNK, :]
        sp = bias
        for grp in range(SGU_GROUPS):
            mixed = _dot(sguw_ref[grp].astype(BF16), vc)
            sp = sp + jnp.where(lane_grp == grp, mixed, 0.0)
        sg_ref[0, ch * SGU_CHUNK:(ch + 1) * SGU_CHUNK, :] = u[ch * SGU_CHUNK:(ch + 1) * SGU_CHUNK, :] * sp


def _inproj_call(h, shift, scale, g, w_ext, gq, gk, cos_t, sin_t, ones_blk, sgun, sguw, sgub, *, rope):
    b, n, d = h.shape
    tm = min(ROW_TILE, n)
    assert n % tm == 0 and tm % SGU_CHUNK == 0
    row = lambda bb, i: (bb, i, 0)
    fixed2 = lambda bb, i: (0, 0)
    outs = (jax.ShapeDtypeStruct((b, n, ATTN_WIDTH), BF16),
            jax.ShapeDtypeStruct((b, n, 256), BF16),
            jax.ShapeDtypeStruct((b, n, 256), BF16),
            jax.ShapeDtypeStruct((b, n, S5_WIDTH), F32),
            jax.ShapeDtypeStruct((b, n, SGU_WIDTH), F32))
    return pl.pallas_call(
        functools.partial(_inproj_kernel, rope=rope),
        out_shape=outs,
        grid=(b, n // tm),
        in_specs=[pl.BlockSpec((1, tm, d), row),
                  pl.BlockSpec((1, 1, d), lambda bb, i: (bb, 0, 0)),
                  pl.BlockSpec((1, 1, d), lambda bb, i: (bb, 0, 0)),
                  pl.BlockSpec((1, d), fixed2),
                  pl.BlockSpec((d, IN_COLS), fixed2),
                  pl.BlockSpec((1, ATTN_WIDTH), fixed2),
                  pl.BlockSpec((1, 256), fixed2),
                  pl.BlockSpec((tm, LANES), lambda bb, i: (i, 0)),
                  pl.BlockSpec((tm, LANES), lambda bb, i: (i, 0)),
                  pl.BlockSpec((ATTN_WIDTH, ATTN_WIDTH), fixed2),
                  pl.BlockSpec((1, SGU_WIDTH), fixed2),
                  pl.BlockSpec((SGU_GROUPS, SGU_CHUNK, SGU_CHUNK), lambda bb, i: (0, 0, 0)),
                  pl.BlockSpec((SGU_CHUNK, SGU_WIDTH), fixed2)],
        out_specs=(pl.BlockSpec((1, tm, ATTN_WIDTH), row),
                   pl.BlockSpec((1, tm, 256), row),
                   pl.BlockSpec((1, tm, 256), row),
                   pl.BlockSpec((1, tm, S5_WIDTH), row),
                   pl.BlockSpec((1, tm, SGU_WIDTH), row)),
        scratch_shapes=[pltpu.VMEM((d, IN_COLS), BF16)],
        compiler_params=_cparams(2),
        name="inproj_rope" if rope else "inproj_ctx",
    )(h, shift, scale, g, w_ext, gq, gk, cos_t, sin_t, ones_blk, sgun, sguw, sgub)


def _stack_group(qj, kh, lane_half):
    pieces = []
    for g in range(KV_GROUP):
        head = kh * KV_GROUP + g
        qc = qj[:, (head // 2) * LANES:(head // 2 + 1) * LANES]
        pieces.append(jnp.where(lane_half == (head % 2), qc, jnp.zeros_like(qc)))
    return jnp.concatenate(pieces, axis=0)


def _sink_col(sink_ref, kh):
    rows = lax.broadcasted_iota(jnp.int32, (KV_GROUP * WINDOW, 1), 0) // WINDOW
    col = jnp.full((KV_GROUP * WINDOW, 1), sink_ref[kh * KV_GROUP], F32)
    for g in range(1, KV_GROUP):
        col = jnp.where(rows == g, sink_ref[kh * KV_GROUP + g], col)
    return col


def _unstack_store(o_ref, o, j, kh, lane_half_f):
    lo = lane_half_f == 0
    r = slice(j * WINDOW, (j + 1) * WINDOW)
    o_ref[0, r, (2 * kh) * LANES:(2 * kh + 1) * LANES] = jnp.where(lo, o[0:128], o[128:256])
    o_ref[0, r, (2 * kh + 1) * LANES:(2 * kh + 2) * LANES] = jnp.where(lo, o[256:384], o[384:512])


def _attn_win_kernel(sink_ref, q_ref, kc_ref, vc_ref, kp_ref, vp_ref, kn_ref, vn_ref, kx_ref, vx_ref, o_ref):
    i = pl.program_id(1)
    nt = pl.num_programs(1)
    tq = q_ref.shape[1]
    nsub = tq // WINDOW
    kfull = jnp.concatenate([kp_ref[0], kc_ref[0], kn_ref[0]], axis=0)
    vfull = jnp.concatenate([vp_ref[0], vc_ref[0], vn_ref[0]], axis=0)
    rows = KV_GROUP * WINDOW
    ri = lax.broadcasted_iota(jnp.int32, (rows, 3 * WINDOW), 0) % WINDOW
    ci = lax.broadcasted_iota(jnp.int32, (rows, 3 * WINDOW), 1)
    cblk = ci // WINDOW
    cj = ci % WINDOW
    ninf = np.float32(-np.inf)
    band = (cblk == 1) | ((cblk == 0) & (cj >= ri)) | ((cblk == 2) & (cj <= ri))
    bias0 = jnp.where(band, 0.0, ninf).astype(F32)
    no_prev = jnp.where(i > 0, 0.0, ninf).astype(F32)
    no_next = jnp.where(i < nt - 1, 0.0, ninf).astype(F32)
    lane_half = lax.broadcasted_iota(jnp.int32, (WINDOW, LANES), 1) // HEAD_DIM
    for j in range(nsub):
        qj = q_ref[0, j * WINDOW:(j + 1) * WINDOW, :]
        bias = bias0
        if j == 0:
            bias = bias + jnp.where(cblk == 0, no_prev, 0.0)
        if j == nsub - 1:
            bias = bias + jnp.where(cblk == 2, no_next, 0.0)
        for kh in range(N_KV_HEADS):
            qs = _stack_group(qj, kh, lane_half)
            ks = slice(kh * LANES, (kh + 1) * LANES)
            kw = kfull[j * WINDOW:(j + 3) * WINDOW, ks]
            vw = vfull[j * WINDOW:(j + 3) * WINDOW, ks]
            s_w = _dot_nt(qs, kw) + bias
            s_c = _dot_nt(qs, kx_ref[0, :, ks])
            sink = _sink_col(sink_ref, kh)
            m = jnp.maximum(jnp.maximum(s_w.max(-1, keepdims=True), s_c.max(-1, keepdims=True)), sink)
            e_w = jnp.exp(s_w - m)
            e_c = jnp.exp(s_c - m)
            den = e_w.sum(-1, keepdims=True) + e_c.sum(-1, keepdims=True) + jnp.exp(sink - m)
            o = _dot(e_w.astype(BF16), vw) + _dot(e_c.astype(BF16), vx_ref[0, :, ks])
            _unstack_store(o_ref, o / den, j, kh, lane_half)


def _attn_ctx_kernel(sink_ref, q_ref, kx_ref, vx_ref, o_ref):
    tq = q_ref.shape[1]
    lane_half = lax.broadcasted_iota(jnp.int32, (WINDOW, LANES), 1) // HEAD_DIM
    for j in range(tq // WINDOW):
        qj = q_ref[0, j * WINDOW:(j + 1) * WINDOW, :]
        for kh in range(N_KV_HEADS):
            qs = _stack_group(qj, kh, lane_half)
            ks = slice(kh * LANES, (kh + 1) * LANES)
            s_c = _dot_nt(qs, kx_ref[0, :, ks])
            sink = _sink_col(sink_ref, kh)
            m = jnp.maximum(s_c.max(-1, keepdims=True), sink)
            e_c = jnp.exp(s_c - m)
            den = e_c.sum(-1, keepdims=True) + jnp.exp(sink - m)
            o = _dot(e_c.astype(BF16), vx_ref[0, :, ks])
            _unstack_store(o_ref, o / den, j, kh, lane_half)


def _attn_win_call(sink, q, k, v, kx, vx):
    b, n, _ = q.shape
    lc = kx.shape[1]
    tq = min(ROW_TILE, n)
    nsub = tq // WINDOW
    nblk = n // WINDOW
    assert n % tq == 0 and tq % WINDOW == 0
    row = lambda bb, i: (bb, i, 0)
    prev = lambda bb, i: (bb, jnp.maximum(i * nsub - 1, 0), 0)
    nxt = lambda bb, i: (bb, jnp.minimum((i + 1) * nsub, nblk - 1), 0)
    ctx = lambda bb, i: (bb, 0, 0)
    return pl.pallas_call(
        _attn_win_kernel,
        out_shape=jax.ShapeDtypeStruct((b, n, ATTN_WIDTH), F32),
        grid=(b, n // tq),
        in_specs=[pl.BlockSpec(memory_space=pltpu.SMEM),
                  pl.BlockSpec((1, tq, ATTN_WIDTH), row),
                  pl.BlockSpec((1, tq, 256), row), pl.BlockSpec((1, tq, 256), row),
                  pl.BlockSpec((1, WINDOW, 256), prev), pl.BlockSpec((1, WINDOW, 256), prev),
                  pl.BlockSpec((1, WINDOW, 256), nxt), pl.BlockSpec((1, WINDOW, 256), nxt),
                  pl.BlockSpec((1, lc, 256), ctx), pl.BlockSpec((1, lc, 256), ctx)],
        out_specs=pl.BlockSpec((1, tq, ATTN_WIDTH), row),
        compiler_params=_cparams(2),
        name="attn_window",
    )(sink, q, k, v, k, v, k, v, kx, vx)


def _attn_ctx_call(sink, q, kx, vx):
    b, lc, _ = q.shape
    assert lc % WINDOW == 0
    blk = lambda bb: (bb, 0, 0)
    return pl.pallas_call(
        _attn_ctx_kernel,
        out_shape=jax.ShapeDtypeStruct((b, lc, ATTN_WIDTH), F32),
        grid=(b,),
        in_specs=[pl.BlockSpec(memory_space=pltpu.SMEM),
                  pl.BlockSpec((1, lc, ATTN_WIDTH), blk),
                  pl.BlockSpec((1, lc, 256), blk), pl.BlockSpec((1, lc, 256), blk)],
        out_specs=pl.BlockSpec((1, lc, ATTN_WIDTH), blk),
        compiler_params=_cparams(1),
        name="attn_ctx",
    )(sink, q, kx, vx)


def _s5_kernel(u_ref, perm_ref, permt_ref, bre_ref, bim_ref, a_ref, al_ref, cre_ref, cim_ref, h0_ref,
               y_ref, hfin_ref, bure_ref, buim_ref, hsre_ref, hsim_ref, carry_ref):
    t = u_ref.shape[1]
    seg_len = t // S5_SEGMENTS

    @pl.when(pl.program_id(1) == 0)
    def _():
        carry_ref[...] = h0_ref[0]

    up = _dot(perm_ref[...], u_ref[0].astype(BF16)).astype(BF16)
    bure_ref[...] = _dot(up, bre_ref[...])
    buim_ref[...] = _dot(up, bim_ref[...])
    a_re = jnp.broadcast_to(a_ref[0:1, :], (S5_SEGMENTS, S5_NSTATE))
    a_im = jnp.broadcast_to(a_ref[1:2, :], (S5_SEGMENTS, S5_NSTATE))

    def step(i, h):
        hr, hi = h
        r0 = pl.multiple_of(i * S5_SEGMENTS, S5_SEGMENTS)
        nr = a_re * hr - a_im * hi + bure_ref[pl.ds(r0, S5_SEGMENTS), :]
        ni = a_re * hi + a_im * hr + buim_ref[pl.ds(r0, S5_SEGMENTS), :]
        return nr, ni

    zero = jnp.zeros((S5_SEGMENTS, S5_NSTATE), F32)
    end_re, end_im = lax.fori_loop(0, seg_len, step, (zero, zero), unroll=S5_UNROLL)

    al_re, al_im = al_ref[0:1, :], al_ref[1:2, :]
    c_re, c_im = carry_ref[0:1, :], carry_ref[1:2, :]
    in_re, in_im = [], []
    for s in range(S5_SEGMENTS):
        in_re.append(c_re)
        in_im.append(c_im)
        c_re, c_im = (al_re * c_re - al_im * c_im + end_re[s:s + 1, :],
                      al_re * c_im + al_im * c_re + end_im[s:s + 1, :])
    carry_ref[0:1, :] = c_re
    carry_ref[1:2, :] = c_im
    hfin_ref[0, 0:1, :] = c_re
    hfin_ref[0, 1:2, :] = c_im

    def step_store(i, h):
        nr, ni = step(i, h)
        r0 = pl.multiple_of(i * S5_SEGMENTS, S5_SEGMENTS)
        hsre_ref[pl.ds(r0, S5_SEGMENTS), :] = nr
        hsim_ref[pl.ds(r0, S5_SEGMENTS), :] = ni
        return nr, ni

    lax.fori_loop(0, seg_len, step_store, (jnp.concatenate(in_re, axis=0), jnp.concatenate(in_im, axis=0)),
                  unroll=S5_UNROLL)

    yp = _dot(hsre_ref[...].astype(BF16), cre_ref[...]) - _dot(hsim_ref[...].astype(BF16), cim_ref[...])
    yh, yl = _split(yp)
    y_ref[0] = _dot(permt_ref[...], yh) + _dot(permt_ref[...], yl)


def _s5_call(u, perm, permt, bre, bim, a, al, cre, cim, h0, *, reverse):
    b, n, w = u.shape
    t = perm.shape[0]
    nc = n // t
    assert n % t == 0
    if reverse:
        row = lambda bb, j: (bb, nc - 1 - j, 0)
    else:
        row = lambda bb, j: (bb, j, 0)
    fixed = lambda bb, j: (0, 0)
    return pl.pallas_call(
        _s5_kernel,
        out_shape=(jax.ShapeDtypeStruct((b, n, w), F32), jax.ShapeDtypeStruct((b, 2, S5_NSTATE), F32)),
        grid=(b, nc),
        in_specs=[pl.BlockSpec((1, t, w), row),
                  pl.BlockSpec((t, t), fixed), pl.BlockSpec((t, t), fixed),
                  pl.BlockSpec((w, S5_NSTATE), fixed), pl.BlockSpec((w, S5_NSTATE), fixed),
                  pl.BlockSpec((2, S5_NSTATE), fixed), pl.BlockSpec((2, S5_NSTATE), fixed),
                  pl.BlockSpec((S5_NSTATE, w), fixed), pl.BlockSpec((S5_NSTATE, w), fixed),
                  pl.BlockSpec((1, 2, S5_NSTATE), lambda bb, j: (bb, 0, 0))],
        out_specs=(pl.BlockSpec((1, t, w), row),
                   pl.BlockSpec((1, 2, S5_NSTATE), lambda bb, j: (bb, 0, 0))),
        scratch_shapes=[pltpu.VMEM((t, S5_NSTATE), F32)] * 4 + [pltpu.VMEM((2, S5_NSTATE), F32)],
        compiler_params=_cparams(2),
        name="s5_scan_rev" if reverse else "s5_scan_fwd",
    )(u, perm, permt, bre, bim, a, al, cre, cim, h0)


def _s5_perm(t, reverse):
    seg_len = t // S5_SEGMENTS
    r = np.arange(t)
    src = (r % S5_SEGMENTS) * seg_len + r // S5_SEGMENTS
    if reverse:
        src = t - 1 - src
    p = np.zeros((t, t), np.float32)
    p[r, src] = 1.0
    return jnp.asarray(p, BF16), jnp.asarray(p.T, BF16)


def _s5_params(lam_re, lam_im, log_dt, b_re, b_im, c_re, c_im, seg_lens):
    lead = lam_re.shape[:-2]
    dt = jnp.exp(log_dt.astype(F32))[..., None]
    lr, li = lam_re.astype(F32), lam_im.astype(F32)
    mag = jnp.exp(lr * dt)
    ang = li * dt
    ab_re, ab_im = mag * jnp.cos(ang), mag * jnp.sin(ang)
    nr, ni = ab_re - 1.0, ab_im
    den = lr * lr + li * li
    coef_re = (nr * lr + ni * li) / den
    coef_im = (ni * lr - nr * li) / den
    br, bi = b_re.astype(F32), b_im.astype(F32)
    bb_re = coef_re[..., None] * br - coef_im[..., None] * bi
    bb_im = coef_re[..., None] * bi + coef_im[..., None] * br
    eye = jnp.eye(S5_GROUPS, dtype=F32)

    def in_mat(bb):
        return jnp.einsum("...gpc,gh->...gchp", bb, eye).reshape(*lead, S5_WIDTH, S5_NSTATE).astype(BF16)

    def out_mat(cc):
        return jnp.einsum("...gcp,gh->...gphc", cc.astype(F32), eye).reshape(*lead, S5_NSTATE, S5_WIDTH).astype(BF16)

    flat = lambda v: v.reshape(*lead, S5_NSTATE)
    a = jnp.stack([flat(ab_re), flat(ab_im)], axis=-2)
    powers, pr, pi, length = {}, ab_re, ab_im, 1
    while length < max(seg_lens):
        pr, pi = pr * pr - pi * pi, 2.0 * pr * pi
        length *= 2
        if length in seg_lens:
            powers[length] = jnp.stack([flat(pr), flat(pi)], axis=-2)
    assert set(powers) == set(seg_lens)
    return in_mat(bb_re), in_mat(bb_im), a, powers, out_mat(c_re), out_mat(c_im)


def _mix_kernel(h_ref, att_ref, yf_ref, yb_ref, us_ref, sg_ref, gate_ref, shift_ref, scale_ref,
                dskip_ref, gluw_ref, glub_ref, outg_ref, wout_ref, gffn_ref, rw_ref,
                hnew_ref, f_ref, p_ref, w16_ref):
    @pl.when((pl.program_id(0) == 0) & (pl.program_id(1) == 0))
    def _():
        w16_ref[...] = wout_ref[0].astype(BF16)

    y = yf_ref[0] + yb_ref[0] + dskip_ref[...] * us_ref[0]
    g = _gelu_tanh(y)
    s5 = g * _sigmoid(_dot(g.astype(BF16), gluw_ref[...]) + glub_ref[...])
    outg = outg_ref[...]
    o1, o2 = ATTN_WIDTH, ATTN_WIDTH + S5_WIDTH
    pa = _rms(att_ref[0], outg[:, :o1]).astype(BF16)
    ps = _rms(s5, outg[:, o1:o2]).astype(BF16)
    pg = _rms(sg_ref[0], outg[:, o2:]).astype(BF16)
    mixed = _dot(pa, w16_ref[:o1, :]) + _dot(ps, w16_ref[o1:o2, :]) + _dot(pg, w16_ref[o2:, :])
    hn = h_ref[0] + gate_ref[0] * mixed
    hnew_ref[0] = hn

    f = _rms(hn, gffn_ref[...]) * (1.0 + scale_ref[0]) + shift_ref[0]
    fh, fl = _split(f)
    f_ref[0] = fh
    rh, rl = _split(rw_ref[...])
    logits = _dot_nt(rh, fh) + _dot_nt(rh, fl) + _dot_nt(rl, fh)
    m = logits.max(axis=0, keepdims=True)
    e = jnp.exp(logits - m)
    p_ref[0] = e / e.sum(axis=0, keepdims=True)


def _mix_call(layer, h, att, yf, yb, us, sg, gate, shift, scale, dskip, gluw, glub, outg, wout, gffn, rwt):
    b, n, d = h.shape
    tm = min(ROW_TILE, n)
    assert n % tm == 0
    row = lambda bb, i: (bb, i, 0)
    vec = lambda bb, i: (bb, 0, 0)
    fixed = lambda bb, i: (0, 0)
    return pl.pallas_call(
        _mix_kernel,
        out_shape=(jax.ShapeDtypeStruct((b, n, d), F32),
                   jax.ShapeDtypeStruct((b, n, d), BF16),
                   jax.ShapeDtypeStruct((b, N_EXPERTS, n), F32)),
        grid=(b, n // tm),
        in_specs=[pl.BlockSpec((1, tm, d), row),
                  pl.BlockSpec((1, tm, ATTN_WIDTH), row),
                  pl.BlockSpec((1, tm, S5_WIDTH), row), pl.BlockSpec((1, tm, S5_WIDTH), row),
                  pl.BlockSpec((1, tm, S5_WIDTH), row), pl.BlockSpec((1, tm, SGU_WIDTH), row),
                  pl.BlockSpec((1, 1, d), vec), pl.BlockSpec((1, 1, d), vec), pl.BlockSpec((1, 1, d), vec),
                  pl.BlockSpec((1, S5_WIDTH), fixed),
                  pl.BlockSpec((S5_WIDTH, S5_WIDTH), fixed),
                  pl.BlockSpec((1, S5_WIDTH), fixed),
                  pl.BlockSpec((1, d), fixed),
                  pl.BlockSpec((1, d, d), lambda bb, i: (layer, 0, 0)),
                  pl.BlockSpec((1, d), fixed),
                  pl.BlockSpec((N_EXPERTS, d), fixed)],
        out_specs=(pl.BlockSpec((1, tm, d), row), pl.BlockSpec((1, tm, d), row),
                   pl.BlockSpec((1, N_EXPERTS, tm), lambda bb, i: (bb, 0, i))),
        scratch_shapes=[pltpu.VMEM((d, d), BF16)],
        compiler_params=_cparams(2),
        name="mix_out_router",
    )(h, att, yf, yb, us, sg, gate, shift, scale, dskip, gluw, glub, outg, wout, gffn, rwt)


def _thr_kernel(p_ref, thr_ref, allow_ref, *, cap):
    bits = pltpu.bitcast(p_ref[0], jnp.int32)

    def count(mask):
        return jnp.sum(mask.astype(F32), axis=1, keepdims=True)

    def body(it, thr):
        cand = thr | jnp.left_shift(jnp.int32(1), 30 - it)
        return jnp.where(count(bits >= cand) >= cap, cand, thr)

    thr = lax.fori_loop(0, 31, body, jnp.zeros((N_EXPERTS, 1), jnp.int32))
    allow = (cap - count(bits > thr)).astype(jnp.int32)
    thr_ref[0] = jnp.broadcast_to(thr, (N_EXPERTS, LANES))
    allow_ref[0] = jnp.broadcast_to(allow, (N_EXPERTS, LANES))


def _thr_call(probs, cap):
    b, e, n = probs.shape
    blk = lambda bb: (bb, 0, 0)
    return pl.pallas_call(
        functools.partial(_thr_kernel, cap=cap),
        out_shape=(jax.ShapeDtypeStruct((b, e, LANES), jnp.int32),) * 2,
        grid=(b,),
        in_specs=[pl.BlockSpec((1, e, n), blk)],
        out_specs=(pl.BlockSpec((1, e, LANES), blk),) * 2,
        compiler_params=_cparams(1),
        name="route_threshold",
    )(probs)


def _rank_kernel(p_ref, thr_ref, allow_ref, rank_ref, offs_ref, eqseen_ref, selseen_ref):
    @pl.when(pl.program_id(1) == 0)
    def _():
        eqseen_ref[...] = jnp.zeros_like(eqseen_ref)
        selseen_ref[...] = jnp.zeros_like(selseen_ref)

    tt = p_ref.shape[2]
    bits = pltpu.bitcast(p_ref[0], jnp.int32)
    thr = thr_ref[0][:, 0:1]
    allow = allow_ref[0][:, 0:1].astype(F32)
    gt = bits > thr
    eq = bits == thr
    before = (lax.broadcasted_iota(jnp.int32, (tt, tt), 0) <
              lax.broadcasted_iota(jnp.int32, (tt, tt), 1)).astype(F32).astype(BF16)
    eq_seen = eqseen_ref[...][:, 0:1]
    eqf = eq.astype(F32)
    eq_before = eq_seen + _dot(eqf.astype(BF16), before)
    sel = gt | (eq & (eq_before < allow))
    self32 = sel.astype(F32)
    sel_seen = selseen_ref[...][:, 0:1]
    rank = sel_seen + _dot(self32.astype(BF16), before)
    rank_ref[0] = jnp.where(sel, rank, -1.0)
    offs_ref[0, 0] = jnp.broadcast_to(sel_seen, (N_EXPERTS, LANES)).astype(jnp.int32)
    eqseen_ref[...] = jnp.broadcast_to(eq_seen + eqf.sum(axis=1, keepdims=True), (N_EXPERTS, LANES))
    selseen_ref[...] = jnp.broadcast_to(sel_seen + self32.sum(axis=1, keepdims=True), (N_EXPERTS, LANES))


def _rank_call(probs, thr, allow, tt):
    b, e, n = probs.shape
    nt = n // tt
    blk = lambda bb, t: (bb, 0, 0)
    return pl.pallas_call(
        _rank_kernel,
        out_shape=(jax.ShapeDtypeStruct((b, e, n), F32),
                   jax.ShapeDtypeStruct((b, nt, e, LANES), jnp.int32)),
        grid=(b, nt),
        in_specs=[pl.BlockSpec((1, e, tt), lambda bb, t: (bb, 0, t)),
                  pl.BlockSpec((1, e, LANES), blk), pl.BlockSpec((1, e, LANES), blk)],
        out_specs=(pl.BlockSpec((1, e, tt), lambda bb, t: (bb, 0, t)),
                   pl.BlockSpec((1, 1, e, LANES), lambda bb, t: (bb, t, 0, 0))),
        scratch_shapes=[pltpu.VMEM((e, LANES), F32)] * 2,
        compiler_params=_cparams(2),
        name="route_rank",
    )(probs, thr, allow)


def _window_rounds(offs_sm, row, base, win):
    o0, o1 = offs_sm[row], offs_sm[row + 1]
    return jnp.where(o1 > o0, (o1 - base + win - 1) // win, 0)


def _window_hit(rank_row, lo, start, iota):
    rel = jnp.where(rank_row >= lo.astype(F32), rank_row, -1.0) - start.astype(F32)
    return rel == iota


def _gather_kernel(offs_sm, rank_ref, p_ref, x_ref, xin_ref, gsel_ref, *, win, tt):
    bb, eg, t = pl.program_id(0), pl.program_id(1), pl.program_id(2)
    grp, cap = xin_ref.shape[1], xin_ref.shape[2]
    sub = x_ref.shape[1] // tt
    n_tiles = pl.num_programs(2) * sub

    @pl.when(t == 0)
    def _():
        xin_ref[...] = jnp.zeros_like(xin_ref)
        gsel_ref[...] = jnp.zeros_like(gsel_ref)

    iota = lax.broadcasted_iota(jnp.int32, (win, tt), 0).astype(F32)
    for s in range(sub):
        cols = slice(s * tt, (s + 1) * tt)
        base, rounds = [], jnp.int32(0)
        for g in range(grp):
            row = ((bb * pl.num_programs(1) + eg) * grp + g) * (n_tiles + 1) + t * sub + s
            base.append((offs_sm[row] // ROW_ALIGN) * ROW_ALIGN)
            rounds = jnp.maximum(rounds, _window_rounds(offs_sm, row, base[g], win))

        def one_round(r, carry, cols=cols, base=base):
            hits, starts = [], []
            for g in range(grp):
                lo = base[g] + r * win
                starts.append(pl.multiple_of(jnp.minimum(lo, cap - win), ROW_ALIGN))
                hits.append(_window_hit(rank_ref[0, pl.ds(eg * grp + g, 1), cols], lo, starts[g], iota))
            onehot = jnp.concatenate([h.astype(F32).astype(BF16) for h in hits], axis=0)
            res = _dot(onehot, x_ref[0, cols, :])
            for g in range(grp):
                rows = pl.ds(starts[g], win)
                xin_ref[0, g, rows, :] += res[g * win:(g + 1) * win].astype(BF16)
                gate = jnp.sum(jnp.where(hits[g], p_ref[0, pl.ds(eg * grp + g, 1), cols], 0.0),
                               axis=1, keepdims=True)
                gsel_ref[0, g, rows, :] += jnp.broadcast_to(gate, (win, LANES))
            return carry

        lax.fori_loop(0, rounds, one_round, 0)


def _gather_call(offs1, rank, probs, f16, cap, win, tt):
    b, n, d = f16.shape
    e, grp = N_EXPERTS, MOE_GROUP
    sub = min(GATHER_TILES, n // tt)
    assert n % (sub * tt) == 0
    grid_spec = pltpu.PrefetchScalarGridSpec(
        num_scalar_prefetch=1,
        grid=(b, e // grp, n // (sub * tt)),
        in_specs=[pl.BlockSpec((1, e, sub * tt), lambda bb, gg, t, o: (bb, 0, t)),
                  pl.BlockSpec((1, e, sub * tt), lambda bb, gg, t, o: (bb, 0, t)),
                  pl.BlockSpec((1, sub * tt, d), lambda bb, gg, t, o: (bb, t, 0))],
        out_specs=(pl.BlockSpec((1, grp, cap, d), lambda bb, gg, t, o: (bb, gg, 0, 0)),
                   pl.BlockSpec((1, grp, cap, LANES), lambda bb, gg, t, o: (bb, gg, 0, 0))),
    )
    return pl.pallas_call(
        functools.partial(_gather_kernel, win=win, tt=tt),
        out_shape=(jax.ShapeDtypeStruct((b, e, cap, d), BF16),
                   jax.ShapeDtypeStruct((b, e, cap, LANES), F32)),
        grid_spec=grid_spec,
        compiler_params=_cparams(3),
        name="moe_gather",
    )(offs1, rank, probs, f16)


def _ffn_kernel(*refs, has_ctx):
    if has_ctx:
        x_ref, g_ref, xc_ref, gc_ref, wg_ref, wu_ref, wd_ref, y_ref, yc_ref, wg16, wu16, wd16 = refs
    else:
        x_ref, g_ref, wg_ref, wu_ref, wd_ref, y_ref, wg16, wu16, wd16 = refs
    m = pl.program_id(2)

    @pl.when((pl.program_id(1) == 0) & (m == 0))
    def _():
        wg16[...] = wg_ref[0, 0].astype(BF16)
        wu16[...] = wu_ref[0, 0].astype(BF16)
        wd16[...] = wd_ref[0, 0].astype(BF16)

    def ffn(x, gate):
        hg = _dot(x, wg16[...])
        hid = (hg * _sigmoid(hg)) * _dot(x, wu16[...])
        return (_dot(hid.astype(BF16), wd16[...]) * gate).astype(BF16)

    if not has_ctx:
        y_ref[0, 0] = ffn(x_ref[0, 0], g_ref[0, 0][:, 0:1])
        return
    tm = x_ref.shape[2]

    @pl.when(m == 0)
    def _():
        x = jnp.concatenate([x_ref[0, 0], xc_ref[0, 0]], axis=0)
        gate = jnp.concatenate([g_ref[0, 0][:, 0:1], gc_ref[0, 0][:, 0:1]], axis=0)
        y = ffn(x, gate)
        y_ref[0, 0] = y[:tm]
        yc_ref[0, 0] = y[tm:]

    @pl.when(m > 0)
    def _():
        y_ref[0, 0] = ffn(x_ref[0, 0], g_ref[0, 0][:, 0:1])


def _ffn_call(layer, xin, gsel, wg, wu, wd, ctx_rows=None):
    b, e, cap, d = xin.shape
    ff = wg.shape[3]
    tm = min(ROW_TILE, cap)
    assert cap % tm == 0
    xrow = lambda ee, bb, m: (bb, ee, m, 0)
    xctx = lambda ee, bb, m: (bb, ee, 0, 0)
    wblk = lambda ee, bb, m: (layer, ee, 0, 0)
    in_specs = [pl.BlockSpec((1, 1, tm, d), xrow), pl.BlockSpec((1, 1, tm, LANES), xrow)]
    out_shape = [jax.ShapeDtypeStruct((b, e, cap, d), BF16)]
    out_specs = [pl.BlockSpec((1, 1, tm, d), xrow)]
    args = [xin, gsel]
    if ctx_rows is not None:
        xc, gc = ctx_rows
        capc = xc.shape[2]
        in_specs += [pl.BlockSpec((1, 1, capc, d), xctx), pl.BlockSpec((1, 1, capc, LANES), xctx)]
        out_shape.append(jax.ShapeDtypeStruct((b, e, capc, d), BF16))
        out_specs.append(pl.BlockSpec((1, 1, capc, d), xctx))
        args += [xc, gc]
    in_specs += [pl.BlockSpec((1, 1, d, ff), wblk), pl.BlockSpec((1, 1, d, ff), wblk),
                 pl.BlockSpec((1, 1, ff, d), wblk)]
    return pl.pallas_call(
        functools.partial(_ffn_kernel, has_ctx=ctx_rows is not None),
        out_shape=tuple(out_shape),
        grid=(e, b, cap // tm),
        in_specs=in_specs,
        out_specs=tuple(out_specs),
        scratch_shapes=[pltpu.VMEM((d, ff), BF16), pltpu.VMEM((d, ff), BF16), pltpu.VMEM((ff, d), BF16)],
        compiler_params=_cparams(3),
        name="moe_expert_ffn",
    )(*args, wg, wu, wd)


def _combine_kernel(offs_sm, rank_ref, y_hbm, h_ref, gate_ref, o_ref, win_ref, sem, *, win):
    bb, t = pl.program_id(0), pl.program_id(1)
    n_b, n_t = pl.num_programs(0), pl.num_programs(1)
    step = bb * n_t + t
    slot = step % 2
    n_e = rank_ref.shape[1]
    cap = y_hbm.shape[2]
    tt = h_ref.shape[1]

    def offs_row(b_i, e, t_i):
        return (b_i * n_e + e) * (n_t + 1) + t_i

    def bases(b_i, t_i):
        return [(offs_sm[offs_row(b_i, e, t_i)] // ROW_ALIGN) * ROW_ALIGN for e in range(n_e)]

    def window_start(lo):
        return pl.multiple_of(jnp.minimum(lo, cap - win), ROW_ALIGN)

    def copies(b_i, base, r, slot_i):
        return [pltpu.make_async_copy(y_hbm.at[b_i, e, pl.ds(window_start(base[e] + r * win), win), :],
                                      win_ref.at[slot_i, pl.ds(e * win, win), :],
                                      sem.at[slot_i, e])
                for e in range(n_e)]

    @pl.when(step == 0)
    def _():
        for cp in copies(bb, bases(bb, t), 0, slot):
            cp.start()

    @pl.when(step + 1 < n_b * n_t)
    def _():
        wrap = t + 1 == n_t
        b2 = jnp.where(wrap, bb + 1, bb)
        t2 = jnp.where(wrap, 0, t + 1)
        for cp in copies(b2, bases(b2, t2), 0, 1 - slot):
            cp.start()

    base = bases(bb, t)
    rounds = jnp.int32(0)
    for e in range(n_e):
        rounds = jnp.maximum(rounds, _window_rounds(offs_sm, offs_row(bb, e, t), base[e], win))
    iota = lax.broadcasted_iota(jnp.int32, (win, tt), 0).astype(F32)

    def contribution(r):
        hits = []
        for e in range(n_e):
            lo = base[e] + r * win
            hit = _window_hit(rank_ref[0, e:e + 1, :], lo, window_start(lo), iota)
            hits.append(hit.astype(F32).astype(BF16))
        return _dot_tn(jnp.concatenate(hits, axis=0), win_ref[slot])

    for cp in copies(bb, base, 0, slot):
        cp.wait()
    o_ref[0] = h_ref[0] + gate_ref[0] * contribution(0)

    def extra_round(r, carry):
        cps = copies(bb, base, r, slot)
        for cp in cps:
            cp.start()
        for cp in cps:
            cp.wait()
        o_ref[0] += gate_ref[0] * contribution(r)
        return carry

    lax.fori_loop(1, rounds, extra_round, 0)


def _combine_call(offs1, rank, y, h, gate, win, tt):
    b, n, d = h.shape
    e = N_EXPERTS
    grid_spec = pltpu.PrefetchScalarGridSpec(
        num_scalar_prefetch=1,
        grid=(b, n // tt),
        in_specs=[pl.BlockSpec((1, e, tt), lambda bb, t, o: (bb, 0, t)),
                  pl.BlockSpec(memory_space=pl.ANY),
                  pl.BlockSpec((1, tt, d), lambda bb, t, o: (bb, t, 0)),
                  pl.BlockSpec((1, 1, d), lambda bb, t, o: (bb, 0, 0))],
        out_specs=pl.BlockSpec((1, tt, d), lambda bb, t, o: (bb, t, 0)),
        scratch_shapes=[pltpu.VMEM((2, e * win, d), BF16), pltpu.SemaphoreType.DMA((2, e))],
    )
    return pl.pallas_call(
        functools.partial(_combine_kernel, win=win),
        out_shape=jax.ShapeDtypeStruct((b, n, d), F32),
        grid_spec=grid_spec,
        compiler_params=_cparams(2),
        name="moe_combine",
    )(offs1, rank, y, h, gate)


def _route(probs):
    b, e, n = probs.shape
    cap = EC_CAPACITY_FACTOR * n // e
    tt = min(ROUTE_TILE, n)
    thr, allow = _thr_call(probs, cap)
    rank, offs = _rank_call(probs, thr, allow, tt)
    offs = jnp.transpose(offs[..., 0], (0, 2, 1))
    offs1 = jnp.concatenate([offs, jnp.full((b, e, 1), cap, jnp.int32)], axis=-1)
    return rank, offs1.reshape(-1), cap, min(ROUTE_BLOCK, cap), tt


def _rope_tables(n):
    rows = n // GRID_W
    row = jnp.repeat(jnp.arange(rows), GRID_W).astype(F32)
    col = jnp.tile(jnp.arange(GRID_W), rows).astype(F32)
    axis_dim = HEAD_DIM // 2
    inv = ROPE_BASE ** (-jnp.arange(0, axis_dim, 2, dtype=F32) / axis_dim)
    ang_r = row[:, None] * inv
    ang_c = col[:, None] * inv
    cr, sr, cc, sc = jnp.cos(ang_r), jnp.sin(ang_r), jnp.cos(ang_c), jnp.sin(ang_c)
    cos64 = jnp.concatenate([cr, cr, cc, cc], axis=1)
    sin64 = jnp.concatenate([-sr, sr, -sc, sc], axis=1)
    return jnp.tile(cos64, (1, 2)), jnp.tile(sin64, (1, 2))


def _extend_w_in(w_in):
    q = w_in[:, :512]
    k = w_in[:, 512:640]
    v = w_in[:, 640:768]
    rest = w_in[:, 768:]
    dup = lambda m: jnp.concatenate([m[:, :64], m[:, :64], m[:, 64:], m[:, 64:]], axis=1)
    return jnp.concatenate([q, dup(k), dup(v), rest], axis=1)


def kernel(x, c, ctx, c_ctx, ada_w, ada_b, norm_mix_g, norm_ffn_g, w_in, q_norm_g, k_norm_g, attn_sink,
           s5_lambda_re, s5_lambda_im, s5_log_dt, s5_b_re, s5_b_im, s5_c_re, s5_c_im, s5_d, s5_glu_w,
           s5_glu_b, sgu_norm_g, sgu_w, sgu_b, out_norm_g, w_out, router_w, exp_w_gate, exp_w_up,
           exp_w_down):
    b, n, d = x.shape
    lc = ctx.shape[1]
    depth = ada_w.shape[0]
    assert b + 1 <= SUBLANES

    crows = jnp.zeros((SUBLANES, d), F32).at[:b].set(c).at[b].set(c_ctx)
    mod = _ada_call(crows, ada_w, ada_b)

    cos_t, sin_t = _rope_tables(n)
    cos_c, sin_c = cos_t[:lc], sin_t[:lc]
    ones_blk = jnp.asarray(np.arange(ATTN_WIDTH)[:, None] // HEAD_DIM == np.arange(ATTN_WIDTH)[None, :] // HEAD_DIM,
                           BF16)
    t_lat = min(ROW_TILE, n)
    t_ctx = min(ROW_TILE, lc)
    perms = {(t, r): _s5_perm(t, r) for t in {t_lat, t_ctx} for r in (False, True)}
    seg_lat, seg_ctx = t_lat // S5_SEGMENTS, t_ctx // S5_SEGMENTS
    s5_bre, s5_bim, s5_a, s5_apow, s5_cre, s5_cim = _s5_params(
        s5_lambda_re, s5_lambda_im, s5_log_dt, s5_b_re, s5_b_im, s5_c_re, s5_c_im, {seg_lat, seg_ctx})

    h_lat, h_ctx = x, ctx
    for l in range(depth):
        ctx_out = l < depth - 1
        mod_lat = mod[l, :b].reshape(b, N_MOD, 1, d)
        mod_ctx = jnp.broadcast_to(mod[l, b].reshape(1, N_MOD, 1, d), (b, N_MOD, 1, d))
        row = lambda v: v.reshape(1, -1)

        w_ext = _extend_w_in(w_in[l])
        gq = row(jnp.tile(q_norm_g[l], N_HEADS) * np.float32(HEAD_DIM ** -0.5))
        gk = row(jnp.tile(k_norm_g[l], 2 * N_KV_HEADS))
        sgub = jnp.repeat(sgu_b[l].T, SGU_WIDTH // SGU_GROUPS, axis=1)
        common = (row(norm_mix_g[l]), w_ext, gq, gk)
        sgu_args = (ones_blk, row(sgu_norm_g[l]), sgu_w[l], sgub)
        q_l, k_l, v_l, s_l, sg_l = _inproj_call(h_lat, mod_lat[:, 0], mod_lat[:, 1], *common, cos_t, sin_t,
                                                *sgu_args, rope=True)
        q_c, k_c, v_c, s_c, sg_c = _inproj_call(h_ctx, mod_ctx[:, 0], mod_ctx[:, 1], *common, cos_c, sin_c,
                                                *sgu_args, rope=False)

        att_lat = _attn_win_call(attn_sink[l], q_l, k_l, v_l, k_c, v_c)

        ys_lat, ys_ctx = [], []
        for direction in range(2):
            rev = direction == 1
            ld = (l, direction)
            bre, bim, a, cre, cim = s5_bre[ld], s5_bim[ld], s5_a[ld], s5_cre[ld], s5_cim[ld]
            h0 = jnp.zeros((b, 2, S5_NSTATE), F32)
            y_c, hfin = _s5_call(s_c, *perms[(t_ctx, rev)], bre, bim, a, s5_apow[seg_ctx][ld], cre, cim, h0,
                                 reverse=rev)
            y_l, _ = _s5_call(s_l, *perms[(t_lat, rev)], bre, bim, a, s5_apow[seg_lat][ld], cre, cim, hfin,
                              reverse=rev)
            ys_lat.append(y_l)
            ys_ctx.append(y_c)

        tail = (row(s5_d[l]), s5_glu_w[l].astype(BF16), row(s5_glu_b[l]), row(out_norm_g[l]), w_out,
                row(norm_ffn_g[l]), router_w[l].T)
        experts = (exp_w_gate, exp_w_up, exp_w_down)
        h_mid, f_lat, p_lat = _mix_call(l, h_lat, att_lat, ys_lat[0], ys_lat[1], s_l, sg_l,
                                        mod_lat[:, 2], mod_lat[:, 3], mod_lat[:, 4], *tail)
        rank_l, offs_l, cap_l, win_l, tt_l = _route(p_lat)
        xin_l, gsel_l = _gather_call(offs_l, rank_l, p_lat, f_lat, cap_l, win_l, tt_l)
        if ctx_out:
            att_ctx = _attn_ctx_call(attn_sink[l], q_c, k_c, v_c)
            hc_mid, f_ctx, p_ctx = _mix_call(l, h_ctx, att_ctx, ys_ctx[0], ys_ctx[1], s_c, sg_c,
                                             mod_ctx[:, 2], mod_ctx[:, 3], mod_ctx[:, 4], *tail)
            rank_c, offs_c, cap_c, win_c, tt_c = _route(p_ctx)
            xin_c, gsel_c = _gather_call(offs_c, rank_c, p_ctx, f_ctx, cap_c, win_c, tt_c)
            y_l, y_c = _ffn_call(l, xin_l, gsel_l, *experts, ctx_rows=(xin_c, gsel_c))
            h_ctx = _combine_call(offs_c, rank_c, y_c, hc_mid, mod_ctx[:, 5], win_c, tt_c)
        else:
            (y_l,) = _ffn_call(l, xin_l, gsel_l, *experts)
        h_lat = _combine_call(offs_l, rank_l, y_l, h_mid, mod_lat[:, 5], win_l, tt_l)
    return h_lat
```

```python
import functools

import numpy as np
import jax
import jax.numpy as jnp
from jax import lax
from jax.experimental import pallas as pl
from jax.experimental.pallas import tpu as pltpu

F32 = jnp.float32
BF16 = jnp.bfloat16

HEAD_DIM = 64
N_HEADS = 8
N_KV_HEADS = 2
KV_GROUP = N_HEADS // N_KV_HEADS
ATTN_WIDTH = N_HEADS * HEAD_DIM
WINDOW = 128
GRID_W = 64
ROPE_BASE = 10000.0
S5_WIDTH = 256
S5_GROUP = 16
S5_GROUPS = S5_WIDTH // S5_GROUP
S5_STATE = 64
S5_NSTATE = S5_GROUPS * S5_STATE
SGU_WIDTH = 256
SGU_GROUPS = 4
SGU_CHUNK = 128
N_EXPERTS = 16
EC_CAPACITY_FACTOR = 2
N_MOD = 6
EPS = 1e-6

LANES = 128
SUBLANES = 8
VMEM_LIMIT_BYTES = 56 * 1024 * 1024

ROW_TILE = 512
FFN_ROW_TILE = 1024
S5_SEGMENTS = SUBLANES
S5_UNROLL = 4
ROUTE_TILE = 512
ROUTE_BLOCK = 128
ROW_ALIGN = 16
MOE_GROUP = 4
GATHER_TILES = 4


def _cparams(n_axes):
    return pltpu.CompilerParams(dimension_semantics=("arbitrary",) * n_axes,
                                vmem_limit_bytes=VMEM_LIMIT_BYTES)


def _dot(a, b):
    return jnp.dot(a, b, preferred_element_type=F32)


def _dot_nt(a, b):
    return lax.dot_general(a, b, (((1,), (1,)), ((), ())), preferred_element_type=F32)


def _dot_tn(a, b):
    return lax.dot_general(a, b, (((0,), (0,)), ((), ())), preferred_element_type=F32)


def _split(x):
    hi = x.astype(BF16)
    lo = (x - hi.astype(F32)).astype(BF16)
    return hi, lo


def _sigmoid(x):
    return 1.0 / (1.0 + jnp.exp(-x))


def _gelu_tanh(x):
    c = np.float32(np.sqrt(2.0 / np.pi))
    return x * (0.5 * (1.0 + jnp.tanh(c * (x + np.float32(0.044715) * (x * x * x)))))


def _rms(x, g):
    return x * lax.rsqrt(jnp.mean(x * x, axis=-1, keepdims=True) + EPS) * g


def _ada_kernel(c_ref, w_ref, b_ref, o_ref):
    c = c_ref[...]
    sc = c * _sigmoid(c)
    ch, cl = _split(sc)
    wh, wl = _split(w_ref[0])
    o_ref[0] = _dot(ch, wh) + _dot(ch, wl) + _dot(cl, wh) + b_ref[0]


def _ada_call(crows, ada_w, ada_b):
    depth, d, nmod = ada_w.shape
    rows = crows.shape[0]
    tn = 1536
    assert nmod % tn == 0
    return pl.pallas_call(
        _ada_kernel,
        out_shape=jax.ShapeDtypeStruct((depth, rows, nmod), F32),
        grid=(depth, nmod // tn),
        in_specs=[pl.BlockSpec((rows, d), lambda l, j: (0, 0)),
                  pl.BlockSpec((1, d, tn), lambda l, j: (l, 0, j)),
                  pl.BlockSpec((1, 1, tn), lambda l, j: (l, 0, j))],
        out_specs=pl.BlockSpec((1, rows, tn), lambda l, j: (l, 0, j)),
        compiler_params=_cparams(2),
        name="ada_mod",
    )(crows, ada_w, ada_b.reshape(depth, 1, nmod))


IN_COLS = ATTN_WIDTH + 2 * (2 * N_KV_HEADS * HEAD_DIM) + S5_WIDTH + 2 * SGU_WIDTH
_QO, _KO, _VO, _SO, _GO = 0, 512, 768, 1024, 1280


def _head_sumsq(x, ones_blk):
    hi, lo = _split(x * x)
    return _dot(hi, ones_blk) + _dot(lo, ones_blk)


def _rope(x, cos, sin, lane_lo):
    outs = []
    for c in range(x.shape[1] // LANES):
        xc = x[:, c * LANES:(c + 1) * LANES]
        up = pltpu.roll(xc, LANES - 16, axis=1)
        dn = pltpu.roll(xc, 16, axis=1)
        partner = jnp.where(lane_lo, up, dn)
        outs.append(xc * cos + partner * sin)
    return jnp.concatenate(outs, axis=1)


def _inproj_kernel(h_ref, shift_ref, scale_ref, g_ref, w_ref, gq_ref, gk_ref, cos_ref, sin_ref,
                   ones_ref, sgun_ref, sguw_ref, sgub_ref,
                   q_ref, k_ref, v_ref, s_ref, sg_ref, w16_ref, *, rope):
    @pl.when((pl.program_id(0) == 0) & (pl.program_id(1) == 0))
    def _():
        w16_ref[...] = w_ref[...].astype(BF16)

    x = h_ref[0]
    a = _rms(x, g_ref[...]) * (1.0 + scale_ref[0]) + shift_ref[0]
    z = _dot(a.astype(BF16), w16_ref[...])
    tm = z.shape[0]

    q = z[:, _QO:_KO]
    k = z[:, _KO:_VO]
    ones_blk = ones_ref[...]
    qn = q * lax.rsqrt(_head_sumsq(q, ones_blk) * (1.0 / HEAD_DIM) + EPS) * gq_ref[...]
    kn = k * lax.rsqrt(_head_sumsq(k, ones_blk[:256, :256]) * (1.0 / HEAD_DIM) + EPS) * gk_ref[...]
    if rope:
        lane_lo = (lax.broadcasted_iota(jnp.int32, (tm, LANES), 1) % 32) < 16
        cos, sin = cos_ref[...], sin_ref[...]
        qn = _rope(qn, cos, sin, lane_lo)
        kn = _rope(kn, cos, sin, lane_lo)
    q_ref[0] = qn.astype(BF16)
    k_ref[0] = kn.astype(BF16)
    v_ref[0] = z[:, _VO:_SO].astype(BF16)
    s_ref[0] = z[:, _SO:_GO]

    gz = _gelu_tanh(z[:, _GO:])
    u = gz[:, :SGU_WIDTH]
    vv = _rms(gz[:, SGU_WIDTH:], sgun_ref[...]).astype(BF16)
    cg = SGU_WIDTH // SGU_GROUPS
    lane_grp = lax.broadcasted_iota(jnp.int32, (SGU_CHUNK, SGU_WIDTH), 1) // cg
    bias = sgub_ref[...]
    for ch in range(tm // SGU_CHUNK):
        vc = vv[ch * SGU_CHUNK:(ch + 1) * SGU_CHUNK, :]
        sp = bias
        for grp in range(SGU_GROUPS):
            mixed = _dot(sguw_ref[grp].astype(BF16), vc)
            sp = sp + jnp.where(lane_grp == grp, mixed, 0.0)
        sg_ref[0, ch * SGU_CHUNK:(ch + 1) * SGU_CHUNK, :] = u[ch * SGU_CHUNK:(ch + 1) * SGU_CHUNK, :] * sp


def _inproj_call(h, shift, scale, g, w_ext, gq, gk, cos_t, sin_t, ones_blk, sgun, sguw, sgub, *, rope):
    b, n, d = h.shape
    tm = min(ROW_TILE, n)
    assert n % tm == 0 and tm % SGU_CHUNK == 0
    row = lambda bb, i: (bb, i, 0)
    fixed2 = lambda bb, i: (0, 0)
    outs = (jax.ShapeDtypeStruct((b, n, ATTN_WIDTH), BF16),
            jax.ShapeDtypeStruct((b, n, 256), BF16),
            jax.ShapeDtypeStruct((b, n, 256), BF16),
            jax.ShapeDtypeStruct((b, n, S5_WIDTH), F32),
            jax.ShapeDtypeStruct((b, n, SGU_WIDTH), F32))
    return pl.pallas_call(
        functools.partial(_inproj_kernel, rope=rope),
        out_shape=outs,
        grid=(b, n // tm),
        in_specs=[pl.BlockSpec((1, tm, d), row),
                  pl.BlockSpec((1, 1, d), lambda bb, i: (bb, 0, 0)),
                  pl.BlockSpec((1, 1, d), lambda bb, i: (bb, 0, 0)),
                  pl.BlockSpec((1, d), fixed2),
                  pl.BlockSpec((d, IN_COLS), fixed2),
                  pl.BlockSpec((1, ATTN_WIDTH), fixed2),
                  pl.BlockSpec((1, 256), fixed2),
                  pl.BlockSpec((tm, LANES), lambda bb, i: (i, 0)),
                  pl.BlockSpec((tm, LANES), lambda bb, i: (i, 0)),
                  pl.BlockSpec((ATTN_WIDTH, ATTN_WIDTH), fixed2),
                  pl.BlockSpec((1, SGU_WIDTH), fixed2),
                  pl.BlockSpec((SGU_GROUPS, SGU_CHUNK, SGU_CHUNK), lambda bb, i: (0, 0, 0)),
                  pl.BlockSpec((SGU_CHUNK, SGU_WIDTH), fixed2)],
        out_specs=(pl.BlockSpec((1, tm, ATTN_WIDTH), row),
                   pl.BlockSpec((1, tm, 256), row),
                   pl.BlockSpec((1, tm, 256), row),
                   pl.BlockSpec((1, tm, S5_WIDTH), row),
                   pl.BlockSpec((1, tm, SGU_WIDTH), row)),
        scratch_shapes=[pltpu.VMEM((d, IN_COLS), BF16)],
        compiler_params=_cparams(2),
        name="inproj_rope" if rope else "inproj_ctx",
    )(h, shift, scale, g, w_ext, gq, gk, cos_t, sin_t, ones_blk, sgun, sguw, sgub)


def _stack_group(qj, kh, lane_half):
    pieces = []
    for g in range(KV_GROUP):
        head = kh * KV_GROUP + g
        qc = qj[:, (head // 2) * LANES:(head // 2 + 1) * LANES]
        pieces.append(jnp.where(lane_half == (head % 2), qc, jnp.zeros_like(qc)))
    return jnp.concatenate(pieces, axis=0)


def _sink_col(sink_ref, kh):
    rows = lax.broadcasted_iota(jnp.int32, (KV_GROUP * WINDOW, 1), 0) // WINDOW
    col = jnp.full((KV_GROUP * WINDOW, 1), sink_ref[kh * KV_GROUP], F32)
    for g in range(1, KV_GROUP):
        col = jnp.where(rows == g, sink_ref[kh * KV_GROUP + g], col)
    return col


def _unstack_store(o_ref, o, j, kh, lane_half_f):
    lo = lane_half_f == 0
    r = slice(j * WINDOW, (j + 1) * WINDOW)
    o_ref[0, r, (2 * kh) * LANES:(2 * kh + 1) * LANES] = jnp.where(lo, o[0:128], o[128:256])
    o_ref[0, r, (2 * kh + 1) * LANES:(2 * kh + 2) * LANES] = jnp.where(lo, o[256:384], o[384:512])


def _attn_win_kernel(sink_ref, q_ref, kc_ref, vc_ref, kp_ref, vp_ref, kn_ref, vn_ref, kx_ref, vx_ref, o_ref):
    i = pl.program_id(1)
    nt = pl.num_programs(1)
    tq = q_ref.shape[1]
    nsub = tq // WINDOW
    kfull = jnp.concatenate([kp_ref[0], kc_ref[0], kn_ref[0]], axis=0)
    vfull = jnp.concatenate([vp_ref[0], vc_ref[0], vn_ref[0]], axis=0)
    rows = KV_GROUP * WINDOW
    ri = lax.broadcasted_iota(jnp.int32, (rows, 3 * WINDOW), 0) % WINDOW
    ci = lax.broadcasted_iota(jnp.int32, (rows, 3 * WINDOW), 1)
    cblk = ci // WINDOW
    cj = ci % WINDOW
    ninf = np.float32(-np.inf)
    band = (cblk == 1) | ((cblk == 0) & (cj >= ri)) | ((cblk == 2) & (cj <= ri))
    bias0 = jnp.where(band, 0.0, ninf).astype(F32)
    no_prev = jnp.where(i > 0, 0.0, ninf).astype(F32)
    no_next = jnp.where(i < nt - 1, 0.0, ninf).astype(F32)
    lane_half = lax.broadcasted_iota(jnp.int32, (WINDOW, LANES), 1) // HEAD_DIM
    for j in range(nsub):
        qj = q_ref[0, j * WINDOW:(j + 1) * WINDOW, :]
        bias = bias0
        if j == 0:
            bias = bias + jnp.where(cblk == 0, no_prev, 0.0)
        if j == nsub - 1:
            bias = bias + jnp.where(cblk == 2, no_next, 0.0)
        for kh in range(N_KV_HEADS):
            qs = _stack_group(qj, kh, lane_half)
            ks = slice(kh * LANES, (kh + 1) * LANES)
            kw = kfull[j * WINDOW:(j + 3) * WINDOW, ks]
            vw = vfull[j * WINDOW:(j + 3) * WINDOW, ks]
            s_w = _dot_nt(qs, kw) + bias
            s_c = _dot_nt(qs, kx_ref[0, :, ks])
            sink = _sink_col(sink_ref, kh)
            m = jnp.maximum(jnp.maximum(s_w.max(-1, keepdims=True), s_c.max(-1, keepdims=True)), sink)
            e_w = jnp.exp(s_w - m)
            e_c = jnp.exp(s_c - m)
            den = e_w.sum(-1, keepdims=True) + e_c.sum(-1, keepdims=True) + jnp.exp(sink - m)
            o = _dot(e_w.astype(BF16), vw) + _dot(e_c.astype(BF16), vx_ref[0, :, ks])
            _unstack_store(o_ref, o / den, j, kh, lane_half)


def _attn_ctx_kernel(sink_ref, q_ref, kx_ref, vx_ref, o_ref):
    tq = q_ref.shape[1]
    lane_half = lax.broadcasted_iota(jnp.int32, (WINDOW, LANES), 1) // HEAD_DIM
    for j in range(tq // WINDOW):
        qj = q_ref[0, j * WINDOW:(j + 1) * WINDOW, :]
        for kh in range(N_KV_HEADS):
            qs = _stack_group(qj, kh, lane_half)
            ks = slice(kh * LANES, (kh + 1) * LANES)
            s_c = _dot_nt(qs, kx_ref[0, :, ks])
            sink = _sink_col(sink_ref, kh)
            m = jnp.maximum(s_c.max(-1, keepdims=True), sink)
            e_c = jnp.exp(s_c - m)
            den = e_c.sum(-1, keepdims=True) + jnp.exp(sink - m)
            o = _dot(e_c.astype(BF16), vx_ref[0, :, ks])
            _unstack_store(o_ref, o / den, j, kh, lane_half)


def _attn_win_call(sink, q, k, v, kx, vx):
    b, n, _ = q.shape
    lc = kx.shape[1]
    tq = min(ROW_TILE, n)
    nsub = tq // WINDOW
    nblk = n // WINDOW
    assert n % tq == 0 and tq % WINDOW == 0
    row = lambda bb, i: (bb, i, 0)
    prev = lambda bb, i: (bb, jnp.maximum(i * nsub - 1, 0), 0)
    nxt = lambda bb, i: (bb, jnp.minimum((i + 1) * nsub, nblk - 1), 0)
    ctx = lambda bb, i: (bb, 0, 0)
    return pl.pallas_call(
        _attn_win_kernel,
        out_shape=jax.ShapeDtypeStruct((b, n, ATTN_WIDTH), F32),
        grid=(b, n // tq),
        in_specs=[pl.BlockSpec(memory_space=pltpu.SMEM),
                  pl.BlockSpec((1, tq, ATTN_WIDTH), row),
                  pl.BlockSpec((1, tq, 256), row), pl.BlockSpec((1, tq, 256), row),
                  pl.BlockSpec((1, WINDOW, 256), prev), pl.BlockSpec((1, WINDOW, 256), prev),
                  pl.BlockSpec((1, WINDOW, 256), nxt), pl.BlockSpec((1, WINDOW, 256), nxt),
                  pl.BlockSpec((1, lc, 256), ctx), pl.BlockSpec((1, lc, 256), ctx)],
        out_specs=pl.BlockSpec((1, tq, ATTN_WIDTH), row),
        compiler_params=_cparams(2),
        name="attn_window",
    )(sink, q, k, v, k, v, k, v, kx, vx)


def _attn_ctx_call(sink, q, kx, vx):
    b, lc, _ = q.shape
    assert lc % WINDOW == 0
    blk = lambda bb: (bb, 0, 0)
    return pl.pallas_call(
        _attn_ctx_kernel,
        out_shape=jax.ShapeDtypeStruct((b, lc, ATTN_WIDTH), F32),
        grid=(b,),
        in_specs=[pl.BlockSpec(memory_space=pltpu.SMEM),
                  pl.BlockSpec((1, lc, ATTN_WIDTH), blk),
                  pl.BlockSpec((1, lc, 256), blk), pl.BlockSpec((1, lc, 256), blk)],
        out_specs=pl.BlockSpec((1, lc, ATTN_WIDTH), blk),
        compiler_params=_cparams(1),
        name="attn_ctx",
    )(sink, q, kx, vx)


def _s5_kernel(u_ref, perm_ref, permt_ref, bre_ref, bim_ref, a_ref, al_ref, cre_ref, cim_ref, h0_ref,
               y_ref, hfin_ref, bure_ref, buim_ref, hsre_ref, hsim_ref, carry_ref):
    t = u_ref.shape[1]
    seg_len = t // S5_SEGMENTS

    @pl.when(pl.program_id(1) == 0)
    def _():
        carry_ref[...] = h0_ref[0]

    up = _dot(perm_ref[...], u_ref[0].astype(BF16)).astype(BF16)
    bure_ref[...] = _dot(up, bre_ref[...])
    buim_ref[...] = _dot(up, bim_ref[...])
    a_re = jnp.broadcast_to(a_ref[0:1, :], (S5_SEGMENTS, S5_NSTATE))
    a_im = jnp.broadcast_to(a_ref[1:2, :], (S5_SEGMENTS, S5_NSTATE))

    def step(i, h):
        hr, hi = h
        r0 = pl.multiple_of(i * S5_SEGMENTS, S5_SEGMENTS)
        nr = a_re * hr - a_im * hi + bure_ref[pl.ds(r0, S5_SEGMENTS), :]
        ni = a_re * hi + a_im * hr + buim_ref[pl.ds(r0, S5_SEGMENTS), :]
        return nr, ni

    zero = jnp.zeros((S5_SEGMENTS, S5_NSTATE), F32)
    end_re, end_im = lax.fori_loop(0, seg_len, step, (zero, zero), unroll=S5_UNROLL)

    al_re, al_im = al_ref[0:1, :], al_ref[1:2, :]
    c_re, c_im = carry_ref[0:1, :], carry_ref[1:2, :]
    in_re, in_im = [], []
    for s in range(S5_SEGMENTS):
        in_re.append(c_re)
        in_im.append(c_im)
        c_re, c_im = (al_re * c_re - al_im * c_im + end_re[s:s + 1, :],
                      al_re * c_im + al_im * c_re + end_im[s:s + 1, :])
    carry_ref[0:1, :] = c_re
    carry_ref[1:2, :] = c_im
    hfin_ref[0, 0:1, :] = c_re
    hfin_ref[0, 1:2, :] = c_im

    def step_store(i, h):
        nr, ni = step(i, h)
        r0 = pl.multiple_of(i * S5_SEGMENTS, S5_SEGMENTS)
        hsre_ref[pl.ds(r0, S5_SEGMENTS), :] = nr
        hsim_ref[pl.ds(r0, S5_SEGMENTS), :] = ni
        return nr, ni

    lax.fori_loop(0, seg_len, step_store, (jnp.concatenate(in_re, axis=0), jnp.concatenate(in_im, axis=0)),
                  unroll=S5_UNROLL)

    yp = _dot(hsre_ref[...].astype(BF16), cre_ref[...]) - _dot(hsim_ref[...].astype(BF16), cim_ref[...])
    yh, yl = _split(yp)
    y_ref[0] = _dot(permt_ref[...], yh) + _dot(permt_ref[...], yl)


def _s5_call(u, perm, permt, bre, bim, a, al, cre, cim, h0, *, reverse):
    b, n, w = u.shape
    t = perm.shape[0]
    nc = n // t
    assert n % t == 0
    if reverse:
        row = lambda bb, j: (bb, nc - 1 - j, 0)
    else:
        row = lambda bb, j: (bb, j, 0)
    fixed = lambda bb, j: (0, 0)
    return pl.pallas_call(
        _s5_kernel,
        out_shape=(jax.ShapeDtypeStruct((b, n, w), F32), jax.ShapeDtypeStruct((b, 2, S5_NSTATE), F32)),
        grid=(b, nc),
        in_specs=[pl.BlockSpec((1, t, w), row),
                  pl.BlockSpec((t, t), fixed), pl.BlockSpec((t, t), fixed),
                  pl.BlockSpec((w, S5_NSTATE), fixed), pl.BlockSpec((w, S5_NSTATE), fixed),
                  pl.BlockSpec((2, S5_NSTATE), fixed), pl.BlockSpec((2, S5_NSTATE), fixed),
                  pl.BlockSpec((S5_NSTATE, w), fixed), pl.BlockSpec((S5_NSTATE, w), fixed),
                  pl.BlockSpec((1, 2, S5_NSTATE), lambda bb, j: (bb, 0, 0))],
        out_specs=(pl.BlockSpec((1, t, w), row),
                   pl.BlockSpec((1, 2, S5_NSTATE), lambda bb, j: (bb, 0, 0))),
        scratch_shapes=[pltpu.VMEM((t, S5_NSTATE), F32)] * 4 + [pltpu.VMEM((2, S5_NSTATE), F32)],
        compiler_params=_cparams(2),
        name="s5_scan_rev" if reverse else "s5_scan_fwd",
    )(u, perm, permt, bre, bim, a, al, cre, cim, h0)


def _s5_perm(t, reverse):
    seg_len = t // S5_SEGMENTS
    r = np.arange(t)
    src = (r % S5_SEGMENTS) * seg_len + r // S5_SEGMENTS
    if reverse:
        src = t - 1 - src
    p = np.zeros((t, t), np.float32)
    p[r, src] = 1.0
    return jnp.asarray(p, BF16), jnp.asarray(p.T, BF16)


def _s5_params(lam_re, lam_im, log_dt, b_re, b_im, c_re, c_im, seg_lens):
    lead = lam_re.shape[:-2]
    dt = jnp.exp(log_dt.astype(F32))[..., None]
    lr, li = lam_re.astype(F32), lam_im.astype(F32)
    mag = jnp.exp(lr * dt)
    ang = li * dt
    ab_re, ab_im = mag * jnp.cos(ang), mag * jnp.sin(ang)
    nr, ni = ab_re - 1.0, ab_im
    den = lr * lr + li * li
    coef_re = (nr * lr + ni * li) / den
    coef_im = (ni * lr - nr * li) / den
    br, bi = b_re.astype(F32), b_im.astype(F32)
    bb_re = coef_re[..., None] * br - coef_im[..., None] * bi
    bb_im = coef_re[..., None] * bi + coef_im[..., None] * br
    eye = jnp.eye(S5_GROUPS, dtype=F32)

    def in_mat(bb):
        return jnp.einsum("...gpc,gh->...gchp", bb, eye).reshape(*lead, S5_WIDTH, S5_NSTATE).astype(BF16)

    def out_mat(cc):
        return jnp.einsum("...gcp,gh->...gphc", cc.astype(F32), eye).reshape(*lead, S5_NSTATE, S5_WIDTH).astype(BF16)

    flat = lambda v: v.reshape(*lead, S5_NSTATE)
    a = jnp.stack([flat(ab_re), flat(ab_im)], axis=-2)
    powers, pr, pi, length = {}, ab_re, ab_im, 1
    while length < max(seg_lens):
        pr, pi = pr * pr - pi * pi, 2.0 * pr * pi
        length *= 2
        if length in seg_lens:
            powers[length] = jnp.stack([flat(pr), flat(pi)], axis=-2)
    assert set(powers) == set(seg_lens)
    return in_mat(bb_re), in_mat(bb_im), a, powers, out_mat(c_re), out_mat(c_im)


def _mix_kernel(h_ref, att_ref, yf_ref, yb_ref, us_ref, sg_ref, gate_ref, shift_ref, scale_ref,
                dskip_ref, gluw_ref, glub_ref, outg_ref, wout_ref, gffn_ref, rw_ref,
                hnew_ref, f_ref, p_ref, w16_ref):
    @pl.when((pl.program_id(0) == 0) & (pl.program_id(1) == 0))
    def _():
        w16_ref[...] = wout_ref[0].astype(BF16)

    y = yf_ref[0] + yb_ref[0] + dskip_ref[...] * us_ref[0]
    g = _gelu_tanh(y)
    s5 = g * _sigmoid(_dot(g.astype(BF16), gluw_ref[...]) + glub_ref[...])
    outg = outg_ref[...]
    o1, o2 = ATTN_WIDTH, ATTN_WIDTH + S5_WIDTH
    pa = _rms(att_ref[0], outg[:, :o1]).astype(BF16)
    ps = _rms(s5, outg[:, o1:o2]).astype(BF16)
    pg = _rms(sg_ref[0], outg[:, o2:]).astype(BF16)
    mixed = _dot(pa, w16_ref[:o1, :]) + _dot(ps, w16_ref[o1:o2, :]) + _dot(pg, w16_ref[o2:, :])
    hn = h_ref[0] + gate_ref[0] * mixed
    hnew_ref[0] = hn

    f = _rms(hn, gffn_ref[...]) * (1.0 + scale_ref[0]) + shift_ref[0]
    fh, fl = _split(f)
    f_ref[0] = fh
    rh, rl = _split(rw_ref[...])
    logits = _dot_nt(rh, fh) + _dot_nt(rh, fl) + _dot_nt(rl, fh)
    m = logits.max(axis=0, keepdims=True)
    e = jnp.exp(logits - m)
    p_ref[0] = e / e.sum(axis=0, keepdims=True)


def _mix_call(layer, h, att, yf, yb, us, sg, gate, shift, scale, dskip, gluw, glub, outg, wout, gffn, rwt):
    b, n, d = h.shape
    tm = min(ROW_TILE, n)
    assert n % tm == 0
    row = lambda bb, i: (bb, i, 0)
    vec = lambda bb, i: (bb, 0, 0)
    fixed = lambda bb, i: (0, 0)
    return pl.pallas_call(
        _mix_kernel,
        out_shape=(jax.ShapeDtypeStruct((b, n, d), F32),
                   jax.ShapeDtypeStruct((b, n, d), BF16),
                   jax.ShapeDtypeStruct((b, N_EXPERTS, n), F32)),
        grid=(b, n // tm),
        in_specs=[pl.BlockSpec((1, tm, d), row),
                  pl.BlockSpec((1, tm, ATTN_WIDTH), row),
                  pl.BlockSpec((1, tm, S5_WIDTH), row), pl.BlockSpec((1, tm, S5_WIDTH), row),
                  pl.BlockSpec((1, tm, S5_WIDTH), row), pl.BlockSpec((1, tm, SGU_WIDTH), row),
                  pl.BlockSpec((1, 1, d), vec), pl.BlockSpec((1, 1, d), vec), pl.BlockSpec((1, 1, d), vec),
                  pl.BlockSpec((1, S5_WIDTH), fixed),
                  pl.BlockSpec((S5_WIDTH, S5_WIDTH), fixed),
                  pl.BlockSpec((1, S5_WIDTH), fixed),
                  pl.BlockSpec((1, d), fixed),
                  pl.BlockSpec((1, d, d), lambda bb, i: (layer, 0, 0)),
                  pl.BlockSpec((1, d), fixed),
                  pl.BlockSpec((N_EXPERTS, d), fixed)],
        out_specs=(pl.BlockSpec((1, tm, d), row), pl.BlockSpec((1, tm, d), row),
                   pl.BlockSpec((1, N_EXPERTS, tm), lambda bb, i: (bb, 0, i))),
        scratch_shapes=[pltpu.VMEM((d, d), BF16)],
        compiler_params=_cparams(2),
        name="mix_out_router",
    )(h, att, yf, yb, us, sg, gate, shift, scale, dskip, gluw, glub, outg, wout, gffn, rwt)


def _thr_kernel(p_ref, thr_ref, allow_ref, *, cap):
    p = p_ref[0]

    def count(mask):
        return jnp.sum(mask.astype(F32), axis=1, keepdims=True)

    def as_float(bits):
        return pltpu.bitcast(jnp.broadcast_to(bits, (N_EXPERTS, LANES)), F32)[:, 0:1]

    def body(it, thr_bits):
        cand = thr_bits | jnp.left_shift(jnp.int32(1), 30 - it)
        return jnp.where(count(p >= as_float(cand)) >= cap, cand, thr_bits)

    thr = as_float(lax.fori_loop(0, 31, body, jnp.zeros((N_EXPERTS, 1), jnp.int32)))
    allow = (cap - count(p > thr)).astype(jnp.int32)
    thr_ref[0] = jnp.broadcast_to(thr, (N_EXPERTS, LANES))
    allow_ref[0] = jnp.broadcast_to(allow, (N_EXPERTS, LANES))


def _thr_call(probs, cap):
    b, e, n = probs.shape
    blk = lambda bb: (bb, 0, 0)
    return pl.pallas_call(
        functools.partial(_thr_kernel, cap=cap),
        out_shape=(jax.ShapeDtypeStruct((b, e, LANES), F32), jax.ShapeDtypeStruct((b, e, LANES), jnp.int32)),
        grid=(b,),
        in_specs=[pl.BlockSpec((1, e, n), blk)],
        out_specs=(pl.BlockSpec((1, e, LANES), blk),) * 2,
        compiler_params=_cparams(1),
        name="route_threshold",
    )(probs)


def _rank_kernel(p_ref, thr_ref, allow_ref, rank_ref, offs_ref, eqseen_ref, selseen_ref):
    @pl.when(pl.program_id(1) == 0)
    def _():
        eqseen_ref[...] = jnp.zeros_like(eqseen_ref)
        selseen_ref[...] = jnp.zeros_like(selseen_ref)

    tt = p_ref.shape[2]
    p = p_ref[0]
    thr = thr_ref[0][:, 0:1]
    allow = allow_ref[0][:, 0:1].astype(F32)
    gt = p > thr
    eq = p == thr
    before = (lax.broadcasted_iota(jnp.int32, (tt, tt), 0) <
              lax.broadcasted_iota(jnp.int32, (tt, tt), 1)).astype(F32).astype(BF16)
    eq_seen = eqseen_ref[...][:, 0:1]
    eqf = eq.astype(F32)
    eq_before = eq_seen + _dot(eqf.astype(BF16), before)
    sel = gt | (eq & (eq_before < allow))
    self32 = sel.astype(F32)
    sel_seen = selseen_ref[...][:, 0:1]
    rank = sel_seen + _dot(self32.astype(BF16), before)
    rank_ref[0] = jnp.where(sel, rank, -1.0)
    offs_ref[0, 0] = jnp.broadcast_to(sel_seen, (N_EXPERTS, LANES)).astype(jnp.int32)
    eqseen_ref[...] = jnp.broadcast_to(eq_seen + eqf.sum(axis=1, keepdims=True), (N_EXPERTS, LANES))
    selseen_ref[...] = jnp.broadcast_to(sel_seen + self32.sum(axis=1, keepdims=True), (N_EXPERTS, LANES))


def _rank_call(probs, thr, allow, tt):
    b, e, n = probs.shape
    nt = n // tt
    blk = lambda bb, t: (bb, 0, 0)
    return pl.pallas_call(
        _rank_kernel,
        out_shape=(jax.ShapeDtypeStruct((b, e, n), F32),
                   jax.ShapeDtypeStruct((b, nt, e, LANES), jnp.int32)),
        grid=(b, nt),
        in_specs=[pl.BlockSpec((1, e, tt), lambda bb, t: (bb, 0, t)),
                  pl.BlockSpec((1, e, LANES), blk), pl.BlockSpec((1, e, LANES), blk)],
        out_specs=(pl.BlockSpec((1, e, tt), lambda bb, t: (bb, 0, t)),
                   pl.BlockSpec((1, 1, e, LANES), lambda bb, t: (bb, t, 0, 0))),
        scratch_shapes=[pltpu.VMEM((e, LANES), F32)] * 2,
        compiler_params=_cparams(2),
        name="route_rank",
    )(probs, thr, allow)


def _window_rounds(offs_sm, row, base, win):
    o0, o1 = offs_sm[row], offs_sm[row + 1]
    return jnp.where(o1 > o0, (o1 - base + win - 1) // win, 0)


def _window_hit(rank_row, lo, start, iota):
    rel = jnp.where(rank_row >= lo.astype(F32), rank_row, -1.0) - start.astype(F32)
    return rel == iota


def _gather_kernel(offs_sm, rank_ref, p_ref, x_ref, xin_ref, gsel_ref, *, win, tt):
    bb, eg, t = pl.program_id(0), pl.program_id(1), pl.program_id(2)
    grp, cap = xin_ref.shape[1], xin_ref.shape[2]
    sub = x_ref.shape[1] // tt
    n_tiles = pl.num_programs(2) * sub

    @pl.when(t == 0)
    def _():
        xin_ref[...] = jnp.zeros_like(xin_ref)
        gsel_ref[...] = jnp.zeros_like(gsel_ref)

    iota = lax.broadcasted_iota(jnp.int32, (win, tt), 0).astype(F32)
    for s in range(sub):
        cols = slice(s * tt, (s + 1) * tt)
        base, rounds = [], jnp.int32(0)
        for g in range(grp):
            row = ((bb * pl.num_programs(1) + eg) * grp + g) * (n_tiles + 1) + t * sub + s
            base.append((offs_sm[row] // ROW_ALIGN) * ROW_ALIGN)
            rounds = jnp.maximum(rounds, _window_rounds(offs_sm, row, base[g], win))

        def one_round(r, carry, cols=cols, base=base):
            hits, starts = [], []
            for g in range(grp):
                lo = base[g] + r * win
                starts.append(pl.multiple_of(jnp.minimum(lo, cap - win), ROW_ALIGN))
                hits.append(_window_hit(rank_ref[0, pl.ds(eg * grp + g, 1), cols], lo, starts[g], iota))
            onehot = jnp.concatenate([h.astype(F32).astype(BF16) for h in hits], axis=0)
            res = _dot(onehot, x_ref[0, cols, :])
            for g in range(grp):
                rows = pl.ds(starts[g], win)
                xin_ref[0, g, rows, :] += res[g * win:(g + 1) * win].astype(BF16)
                gate = jnp.sum(jnp.where(hits[g], p_ref[0, pl.ds(eg * grp + g, 1), cols], 0.0),
                               axis=1, keepdims=True)
                gsel_ref[0, g, rows, :] += jnp.broadcast_to(gate, (win, LANES))
            return carry

        lax.fori_loop(0, rounds, one_round, 0)


def _gather_call(offs1, rank, probs, f16, cap, win, tt):
    b, n, d = f16.shape
    e, grp = N_EXPERTS, MOE_GROUP
    sub = min(GATHER_TILES, n // tt)
    assert n % (sub * tt) == 0
    grid_spec = pltpu.PrefetchScalarGridSpec(
        num_scalar_prefetch=1,
        grid=(b, e // grp, n // (sub * tt)),
        in_specs=[pl.BlockSpec((1, e, sub * tt), lambda bb, gg, t, o: (bb, 0, t)),
                  pl.BlockSpec((1, e, sub * tt), lambda bb, gg, t, o: (bb, 0, t)),
                  pl.BlockSpec((1, sub * tt, d), lambda bb, gg, t, o: (bb, t, 0))],
        out_specs=(pl.BlockSpec((1, grp, cap, d), lambda bb, gg, t, o: (bb, gg, 0, 0)),
                   pl.BlockSpec((1, grp, cap, LANES), lambda bb, gg, t, o: (bb, gg, 0, 0))),
    )
    return pl.pallas_call(
        functools.partial(_gather_kernel, win=win, tt=tt),
        out_shape=(jax.ShapeDtypeStruct((b, e, cap, d), BF16),
                   jax.ShapeDtypeStruct((b, e, cap, LANES), F32)),
        grid_spec=grid_spec,
        compiler_params=_cparams(3),
        name="moe_gather",
    )(offs1, rank, probs, f16)


def _ffn_kernel(*refs, has_ctx, layer):
    if has_ctx:
        x_ref, g_ref, xc_ref, gc_ref, wg_hbm, wu_hbm, wd_hbm, y_ref, yc_ref, w32, w16, sem = refs
    else:
        x_ref, g_ref, wg_hbm, wu_hbm, wd_hbm, y_ref, w32, w16, sem = refs
    e, m = pl.program_id(0), pl.program_id(2)
    n_e = pl.num_programs(0)

    def weight_copies(expert):
        return [pltpu.make_async_copy(w_hbm.at[layer, expert], w32.at[k], sem.at[k])
                for k, w_hbm in enumerate((wg_hbm, wu_hbm, wd_hbm))]

    @pl.when((pl.program_id(1) == 0) & (m == 0))
    def _():
        @pl.when(e == 0)
        def _():
            for cp in weight_copies(e):
                cp.start()

        for cp in weight_copies(e):
            cp.wait()
        for k in range(3):
            w16[k] = w32[k].astype(BF16)

        @pl.when(e + 1 < n_e)
        def _():
            for cp in weight_copies(e + 1):
                cp.start()

    def ffn(x, gate):
        hg = _dot(x, w16[0])
        hid = (hg * _sigmoid(hg)) * _dot(x, w16[1])
        return (_dot(hid.astype(BF16), w16[2]) * gate).astype(BF16)

    if not has_ctx:
        y_ref[0, 0] = ffn(x_ref[0, 0], g_ref[0, 0][:, 0:1])
        return
    tm = x_ref.shape[2]

    @pl.when(m == 0)
    def _():
        x = jnp.concatenate([x_ref[0, 0], xc_ref[0, 0]], axis=0)
        gate = jnp.concatenate([g_ref[0, 0][:, 0:1], gc_ref[0, 0][:, 0:1]], axis=0)
        y = ffn(x, gate)
        y_ref[0, 0] = y[:tm]
        yc_ref[0, 0] = y[tm:]

    @pl.when(m > 0)
    def _():
        y_ref[0, 0] = ffn(x_ref[0, 0], g_ref[0, 0][:, 0:1])


def _ffn_call(layer, xin, gsel, wg, wu, wd, ctx_rows=None):
    b, e, cap, d = xin.shape
    ff = wg.shape[3]
    assert wg.shape[2:] == (d, ff) and wu.shape[2:] == (d, ff) and wd.shape[2:] == (ff, d) and ff == d
    tm = min(FFN_ROW_TILE, cap)
    assert cap % tm == 0
    xrow = lambda ee, bb, m: (bb, ee, m, 0)
    xctx = lambda ee, bb, m: (bb, ee, 0, 0)
    in_specs = [pl.BlockSpec((1, 1, tm, d), xrow), pl.BlockSpec((1, 1, tm, LANES), xrow)]
    out_shape = [jax.ShapeDtypeStruct((b, e, cap, d), BF16)]
    out_specs = [pl.BlockSpec((1, 1, tm, d), xrow)]
    args = [xin, gsel]
    if ctx_rows is not None:
        xc, gc = ctx_rows
        capc = xc.shape[2]
        in_specs += [pl.BlockSpec((1, 1, capc, d), xctx), pl.BlockSpec((1, 1, capc, LANES), xctx)]
        out_shape.append(jax.ShapeDtypeStruct((b, e, capc, d), BF16))
        out_specs.append(pl.BlockSpec((1, 1, capc, d), xctx))
        args += [xc, gc]
    in_specs += [pl.BlockSpec(memory_space=pl.ANY)] * 3
    return pl.pallas_call(
        functools.partial(_ffn_kernel, has_ctx=ctx_rows is not None, layer=layer),
        out_shape=tuple(out_shape),
        grid=(e, b, cap // tm),
        in_specs=in_specs,
        out_specs=tuple(out_specs),
        scratch_shapes=[pltpu.VMEM((3, d, ff), F32), pltpu.VMEM((3, d, ff), BF16), pltpu.SemaphoreType.DMA((3,))],
        compiler_params=_cparams(3),
        name="moe_expert_ffn",
    )(*args, wg, wu, wd)


def _combine_kernel(offs_sm, rank_ref, y_hbm, h_ref, gate_ref, o_ref, win_ref, sem, *, win):
    bb, t = pl.program_id(0), pl.program_id(1)
    n_b, n_t = pl.num_programs(0), pl.num_programs(1)
    step = bb * n_t + t
    slot = step % 2
    n_e = rank_ref.shape[1]
    cap = y_hbm.shape[2]
    tt = h_ref.shape[1]

    def offs_row(b_i, e, t_i):
        return (b_i * n_e + e) * (n_t + 1) + t_i

    def bases(b_i, t_i):
        return [(offs_sm[offs_row(b_i, e, t_i)] // ROW_ALIGN) * ROW_ALIGN for e in range(n_e)]

    def window_start(lo):
        return pl.multiple_of(jnp.minimum(lo, cap - win), ROW_ALIGN)

    def copies(b_i, base, r, slot_i):
        return [pltpu.make_async_copy(y_hbm.at[b_i, e, pl.ds(window_start(base[e] + r * win), win), :],
                                      win_ref.at[slot_i, pl.ds(e * win, win), :],
                                      sem.at[slot_i, e])
                for e in range(n_e)]

    @pl.when(step == 0)
    def _():
        for cp in copies(bb, bases(bb, t), 0, slot):
            cp.start()

    @pl.when(step + 1 < n_b * n_t)
    def _():
        wrap = t + 1 == n_t
        b2 = jnp.where(wrap, bb + 1, bb)
        t2 = jnp.where(wrap, 0, t + 1)
        for cp in copies(b2, bases(b2, t2), 0, 1 - slot):
            cp.start()

    base = bases(bb, t)
    rounds = jnp.int32(0)
    for e in range(n_e):
        rounds = jnp.maximum(rounds, _window_rounds(offs_sm, offs_row(bb, e, t), base[e], win))
    iota = lax.broadcasted_iota(jnp.int32, (win, tt), 0).astype(F32)

    def contribution(r):
        hits = []
        for e in range(n_e):
            lo = base[e] + r * win
            hit = _window_hit(rank_ref[0, e:e + 1, :], lo, window_start(lo), iota)
            hits.append(hit.astype(F32).astype(BF16))
        return _dot_tn(jnp.concatenate(hits, axis=0), win_ref[slot])

    for cp in copies(bb, base, 0, slot):
        cp.wait()
    o_ref[0] = h_ref[0] + gate_ref[0] * contribution(0)

    def extra_round(r, carry):
        cps = copies(bb, base, r, slot)
        for cp in cps:
            cp.start()
        for cp in cps:
            cp.wait()
        o_ref[0] += gate_ref[0] * contribution(r)
        return carry

    lax.fori_loop(1, rounds, extra_round, 0)


def _combine_call(offs1, rank, y, h, gate, win, tt):
    b, n, d = h.shape
    e = N_EXPERTS
    grid_spec = pltpu.PrefetchScalarGridSpec(
        num_scalar_prefetch=1,
        grid=(b, n // tt),
        in_specs=[pl.BlockSpec((1, e, tt), lambda bb, t, o: (bb, 0, t)),
                  pl.BlockSpec(memory_space=pl.ANY),
                  pl.BlockSpec((1, tt, d), lambda bb, t, o: (bb, t, 0)),
                  pl.BlockSpec((1, 1, d), lambda bb, t, o: (bb, 0, 0))],
        out_specs=pl.BlockSpec((1, tt, d), lambda bb, t, o: (bb, t, 0)),
        scratch_shapes=[pltpu.VMEM((2, e * win, d), BF16), pltpu.SemaphoreType.DMA((2, e))],
    )
    return pl.pallas_call(
        functools.partial(_combine_kernel, win=win),
        out_shape=jax.ShapeDtypeStruct((b, n, d), F32),
        grid_spec=grid_spec,
        compiler_params=_cparams(2),
        name="moe_combine",
    )(offs1, rank, y, h, gate)


def _route(probs):
    b, e, n = probs.shape
    cap = EC_CAPACITY_FACTOR * n // e
    tt = min(ROUTE_TILE, n)
    thr, allow = _thr_call(probs, cap)
    rank, offs = _rank_call(probs, thr, allow, tt)
    offs = jnp.transpose(offs[..., 0], (0, 2, 1))
    offs1 = jnp.concatenate([offs, jnp.full((b, e, 1), cap, jnp.int32)], axis=-1)
    return rank, offs1.reshape(-1), cap, min(ROUTE_BLOCK, cap), tt


def _rope_tables(n):
    rows = n // GRID_W
    row = jnp.repeat(jnp.arange(rows), GRID_W).astype(F32)
    col = jnp.tile(jnp.arange(GRID_W), rows).astype(F32)
    axis_dim = HEAD_DIM // 2
    inv = ROPE_BASE ** (-jnp.arange(0, axis_dim, 2, dtype=F32) / axis_dim)
    ang_r = row[:, None] * inv
    ang_c = col[:, None] * inv
    cr, sr, cc, sc = jnp.cos(ang_r), jnp.sin(ang_r), jnp.cos(ang_c), jnp.sin(ang_c)
    cos64 = jnp.concatenate([cr, cr, cc, cc], axis=1)
    sin64 = jnp.concatenate([-sr, sr, -sc, sc], axis=1)
    return jnp.tile(cos64, (1, 2)), jnp.tile(sin64, (1, 2))


def _extend_w_in(w_in):
    q = w_in[:, :512]
    k = w_in[:, 512:640]
    v = w_in[:, 640:768]
    rest = w_in[:, 768:]
    dup = lambda m: jnp.concatenate([m[:, :64], m[:, :64], m[:, 64:], m[:, 64:]], axis=1)
    return jnp.concatenate([q, dup(k), dup(v), rest], axis=1)


def kernel(x, c, ctx, c_ctx, ada_w, ada_b, norm_mix_g, norm_ffn_g, w_in, q_norm_g, k_norm_g, attn_sink,
           s5_lambda_re, s5_lambda_im, s5_log_dt, s5_b_re, s5_b_im, s5_c_re, s5_c_im, s5_d, s5_glu_w,
           s5_glu_b, sgu_norm_g, sgu_w, sgu_b, out_norm_g, w_out, router_w, exp_w_gate, exp_w_up,
           exp_w_down):
    b, n, d = x.shape
    lc = ctx.shape[1]
    depth = ada_w.shape[0]
    assert b + 1 <= SUBLANES

    crows = jnp.zeros((SUBLANES, d), F32).at[:b].set(c).at[b].set(c_ctx)
    mod = _ada_call(crows, ada_w, ada_b)

    cos_t, sin_t = _rope_tables(n)
    cos_c, sin_c = cos_t[:lc], sin_t[:lc]
    ones_blk = jnp.asarray(np.arange(ATTN_WIDTH)[:, None] // HEAD_DIM == np.arange(ATTN_WIDTH)[None, :] // HEAD_DIM,
                           BF16)
    t_lat = min(ROW_TILE, n)
    t_ctx = min(ROW_TILE, lc)
    perms = {(t, r): _s5_perm(t, r) for t in {t_lat, t_ctx} for r in (False, True)}
    seg_lat, seg_ctx = t_lat // S5_SEGMENTS, t_ctx // S5_SEGMENTS
    s5_bre, s5_bim, s5_a, s5_apow, s5_cre, s5_cim = _s5_params(
        s5_lambda_re, s5_lambda_im, s5_log_dt, s5_b_re, s5_b_im, s5_c_re, s5_c_im, {seg_lat, seg_ctx})

    h_lat, h_ctx = x, ctx
    for l in range(depth):
        ctx_out = l < depth - 1
        mod_lat = mod[l, :b].reshape(b, N_MOD, 1, d)
        mod_ctx = jnp.broadcast_to(mod[l, b].reshape(1, N_MOD, 1, d), (b, N_MOD, 1, d))
        row = lambda v: v.reshape(1, -1)

        w_ext = _extend_w_in(w_in[l])
        gq = row(jnp.tile(q_norm_g[l], N_HEADS) * np.float32(HEAD_DIM ** -0.5))
        gk = row(jnp.tile(k_norm_g[l], 2 * N_KV_HEADS))
        sgub = jnp.repeat(sgu_b[l].T, SGU_WIDTH // SGU_GROUPS, axis=1)
        common = (row(norm_mix_g[l]), w_ext, gq, gk)
        sgu_args = (ones_blk, row(sgu_norm_g[l]), sgu_w[l], sgub)
        q_l, k_l, v_l, s_l, sg_l = _inproj_call(h_lat, mod_lat[:, 0], mod_lat[:, 1], *common, cos_t, sin_t,
                                                *sgu_args, rope=True)
        q_c, k_c, v_c, s_c, sg_c = _inproj_call(h_ctx, mod_ctx[:, 0], mod_ctx[:, 1], *common, cos_c, sin_c,
                                                *sgu_args, rope=False)

        att_lat = _attn_win_call(attn_sink[l], q_l, k_l, v_l, k_c, v_c)

        ys_lat, ys_ctx = [], []
        for direction in range(2):
            rev = direction == 1
            ld = (l, direction)
            bre, bim, a, cre, cim = s5_bre[ld], s5_bim[ld], s5_a[ld], s5_cre[ld], s5_cim[ld]
            h0 = jnp.zeros((b, 2, S5_NSTATE), F32)
            y_c, hfin = _s5_call(s_c, *perms[(t_ctx, rev)], bre, bim, a, s5_apow[seg_ctx][ld], cre, cim, h0,
                                 reverse=rev)
            y_l, _ = _s5_call(s_l, *perms[(t_lat, rev)], bre, bim, a, s5_apow[seg_lat][ld], cre, cim, hfin,
                              reverse=rev)
            ys_lat.append(y_l)
            ys_ctx.append(y_c)

        tail = (row(s5_d[l]), s5_glu_w[l].astype(BF16), row(s5_glu_b[l]), row(out_norm_g[l]), w_out,
                row(norm_ffn_g[l]), router_w[l].T)
        experts = (exp_w_gate, exp_w_up, exp_w_down)
        h_mid, f_lat, p_lat = _mix_call(l, h_lat, att_lat, ys_lat[0], ys_lat[1], s_l, sg_l,
                                        mod_lat[:, 2], mod_lat[:, 3], mod_lat[:, 4], *tail)
        rank_l, offs_l, cap_l, win_l, tt_l = _route(p_lat)
        xin_l, gsel_l = _gather_call(offs_l, rank_l, p_lat, f_lat, cap_l, win_l, tt_l)
        if ctx_out:
            att_ctx = _attn_ctx_call(attn_sink[l], q_c, k_c, v_c)
            hc_mid, f_ctx, p_ctx = _mix_call(l, h_ctx, att_ctx, ys_ctx[0], ys_ctx[1], s_c, sg_c,
                                             mod_ctx[:, 2], mod_ctx[:, 3], mod_ctx[:, 4], *tail)
            rank_c, offs_c, cap_c, win_c, tt_c = _route(p_ctx)
            xin_c, gsel_c = _gather_call(offs_c, rank_c, p_ctx, f_ctx, cap_c, win_c, tt_c)
            y_l, y_c = _ffn_call(l, xin_l, gsel_l, *experts, ctx_rows=(xin_c, gsel_c))
            h_ctx = _combine_call(offs_c, rank_c, y_c, hc_mid, mod_ctx[:, 5], win_c, tt_c)
        else:
            (y_l,) = _ffn_call(l, xin_l, gsel_l, *experts)
        h_lat = _combine_call(offs_l, rank_l, y_l, h_mid, mod_lat[:, 5], win_l, tt_l)
    return h_lat
```

```python
import functools

import numpy as np
import jax
import jax.numpy as jnp
from jax import lax
from jax.experimental import pallas as pl
from jax.experimental.pallas import tpu as pltpu

F32 = jnp.float32
BF16 = jnp.bfloat16

HEAD_DIM = 64
N_HEADS = 8
N_KV_HEADS = 2
KV_GROUP = N_HEADS // N_KV_HEADS
ATTN_WIDTH = N_HEADS * HEAD_DIM
WINDOW = 128
GRID_W = 64
ROPE_BASE = 10000.0
S5_WIDTH = 256
S5_GROUP = 16
S5_GROUPS = S5_WIDTH // S5_GROUP
S5_STATE = 64
S5_NSTATE = S5_GROUPS * S5_STATE
SGU_WIDTH = 256
SGU_GROUPS = 4
SGU_CHUNK = 128
N_EXPERTS = 16
EC_CAPACITY_FACTOR = 2
N_MOD = 6
EPS = 1e-6

LANES = 128
SUBLANES = 8
VMEM_LIMIT_BYTES = 56 * 1024 * 1024

ROW_TILE = 1024
FFN_ROW_TILE = 1024
S5_CHUNK = 512
S5_SEGMENTS = SUBLANES
S5_UNROLL = 4
ROUTE_TILE = 512
ROUTE_BLOCK = 128
ROW_ALIGN = 16
MOE_GROUP = 4
GATHER_TILES = 4


def _cparams(n_axes):
    return pltpu.CompilerParams(dimension_semantics=("arbitrary",) * n_axes,
                                vmem_limit_bytes=VMEM_LIMIT_BYTES)


def _dot(a, b):
    return jnp.dot(a, b, preferred_element_type=F32)


def _dot_nt(a, b):
    return lax.dot_general(a, b, (((1,), (1,)), ((), ())), preferred_element_type=F32)


def _dot_tn(a, b):
    return lax.dot_general(a, b, (((0,), (0,)), ((), ())), preferred_element_type=F32)


def _split(x):
    hi = x.astype(BF16)
    lo = (x - hi.astype(F32)).astype(BF16)
    return hi, lo


def _sigmoid(x):
    return 1.0 / (1.0 + jnp.exp(-x))


def _gelu_tanh(x):
    c = np.float32(np.sqrt(2.0 / np.pi))
    return x * (0.5 * (1.0 + jnp.tanh(c * (x + np.float32(0.044715) * (x * x * x)))))


def _rms(x, g):
    return x * lax.rsqrt(jnp.mean(x * x, axis=-1, keepdims=True) + EPS) * g


def _ada_kernel(c_ref, w_ref, b_ref, o_ref):
    c = c_ref[...]
    sc = c * _sigmoid(c)
    ch, cl = _split(sc)
    wh, wl = _split(w_ref[0])
    o_ref[0] = _dot(ch, wh) + _dot(ch, wl) + _dot(cl, wh) + b_ref[0]


def _ada_call(crows, ada_w, ada_b):
    depth, d, nmod = ada_w.shape
    rows = crows.shape[0]
    tn = 1536
    assert nmod % tn == 0
    return pl.pallas_call(
        _ada_kernel,
        out_shape=jax.ShapeDtypeStruct((depth, rows, nmod), F32),
        grid=(depth, nmod // tn),
        in_specs=[pl.BlockSpec((rows, d), lambda l, j: (0, 0)),
                  pl.BlockSpec((1, d, tn), lambda l, j: (l, 0, j)),
                  pl.BlockSpec((1, 1, tn), lambda l, j: (l, 0, j))],
        out_specs=pl.BlockSpec((1, rows, tn), lambda l, j: (l, 0, j)),
        compiler_params=_cparams(2),
        name="ada_mod",
    )(crows, ada_w, ada_b.reshape(depth, 1, nmod))


IN_COLS = ATTN_WIDTH + 2 * (2 * N_KV_HEADS * HEAD_DIM) + S5_WIDTH + 2 * SGU_WIDTH
_QO, _KO, _VO, _SO, _GO = 0, 512, 768, 1024, 1280


def _head_sumsq(x, ones_blk):
    hi, lo = _split(x * x)
    return _dot(hi, ones_blk) + _dot(lo, ones_blk)


def _rope(x, cos, sin, lane_lo):
    outs = []
    for c in range(x.shape[1] // LANES):
        xc = x[:, c * LANES:(c + 1) * LANES]
        up = pltpu.roll(xc, LANES - 16, axis=1)
        dn = pltpu.roll(xc, 16, axis=1)
        partner = jnp.where(lane_lo, up, dn)
        outs.append(xc * cos + partner * sin)
    return jnp.concatenate(outs, axis=1)


def _inproj_kernel(h_ref, shift_ref, scale_ref, g_ref, w_ref, gq_ref, gk_ref, cos_ref, sin_ref,
                   ones_ref, sgun_ref, sguw_ref, sgub_ref,
                   q_ref, k_ref, v_ref, s_ref, sg_ref, w16_ref, *, rope):
    @pl.when((pl.program_id(0) == 0) & (pl.program_id(1) == 0))
    def _():
        w16_ref[...] = w_ref[...].astype(BF16)

    x = h_ref[0]
    a = _rms(x, g_ref[...]) * (1.0 + scale_ref[0]) + shift_ref[0]
    z = _dot(a.astype(BF16), w16_ref[...])
    tm = z.shape[0]

    q = z[:, _QO:_KO]
    k = z[:, _KO:_VO]
    ones_blk = ones_ref[...]
    qn = q * lax.rsqrt(_head_sumsq(q, ones_blk) * (1.0 / HEAD_DIM) + EPS) * gq_ref[...]
    kn = k * lax.rsqrt(_head_sumsq(k, ones_blk[:256, :256]) * (1.0 / HEAD_DIM) + EPS) * gk_ref[...]
    if rope:
        lane_lo = (lax.broadcasted_iota(jnp.int32, (tm, LANES), 1) % 32) < 16
        cos, sin = cos_ref[...], sin_ref[...]
        qn = _rope(qn, cos, sin, lane_lo)
        kn = _rope(kn, cos, sin, lane_lo)
    q_ref[0] = qn.astype(BF16)
    k_ref[0] = kn.astype(BF16)
    v_ref[0] = z[:, _VO:_SO].astype(BF16)
    s_ref[0] = z[:, _SO:_GO]

    gz = _gelu_tanh(z[:, _GO:])
    u = gz[:, :SGU_WIDTH]
    vv = _rms(gz[:, SGU_WIDTH:], sgun_ref[...]).astype(BF16)
    cg = SGU_WIDTH // SGU_GROUPS
    lane_grp = lax.broadcasted_iota(jnp.int32, (SGU_CHUNK, SGU_WIDTH), 1) // cg
    bias = sgub_ref[...]
    for ch in range(tm // SGU_CHUNK):
        vc = vv[ch * SGU_CHUNK:(ch + 1) * SGU_CHUNK, :]
        sp = bias
        for grp in range(SGU_GROUPS):
            mixed = _dot(sguw_ref[grp].astype(BF16), vc)
            sp = sp + jnp.where(lane_grp == grp, mixed, 0.0)
        sg_ref[0, ch * SGU_CHUNK:(ch + 1) * SGU_CHUNK, :] = u[ch * SGU_CHUNK:(ch + 1) * SGU_CHUNK, :] * sp


def _inproj_call(h, shift, scale, g, w_ext, gq, gk, cos_t, sin_t, ones_blk, sgun, sguw, sgub, *, rope):
    b, n, d = h.shape
    tm = min(ROW_TILE, n)
    assert n % tm == 0 and tm % SGU_CHUNK == 0
    row = lambda bb, i: (bb, i, 0)
    fixed2 = lambda bb, i: (0, 0)
    outs = (jax.ShapeDtypeStruct((b, n, ATTN_WIDTH), BF16),
            jax.ShapeDtypeStruct((b, n, 256), BF16),
            jax.ShapeDtypeStruct((b, n, 256), BF16),
            jax.ShapeDtypeStruct((b, n, S5_WIDTH), F32),
            jax.ShapeDtypeStruct((b, n, SGU_WIDTH), F32))
    return pl.pallas_call(
        functools.partial(_inproj_kernel, rope=rope),
        out_shape=outs,
        grid=(b, n // tm),
        in_specs=[pl.BlockSpec((1, tm, d), row),
                  pl.BlockSpec((1, 1, d), lambda bb, i: (bb, 0, 0)),
                  pl.BlockSpec((1, 1, d), lambda bb, i: (bb, 0, 0)),
                  pl.BlockSpec((1, d), fixed2),
                  pl.BlockSpec((d, IN_COLS), fixed2),
                  pl.BlockSpec((1, ATTN_WIDTH), fixed2),
                  pl.BlockSpec((1, 256), fixed2),
                  pl.BlockSpec((tm, LANES), lambda bb, i: (i, 0)),
                  pl.BlockSpec((tm, LANES), lambda bb, i: (i, 0)),
                  pl.BlockSpec((ATTN_WIDTH, ATTN_WIDTH), fixed2),
                  pl.BlockSpec((1, SGU_WIDTH), fixed2),
                  pl.BlockSpec((SGU_GROUPS, SGU_CHUNK, SGU_CHUNK), lambda bb, i: (0, 0, 0)),
                  pl.BlockSpec((SGU_CHUNK, SGU_WIDTH), fixed2)],
        out_specs=(pl.BlockSpec((1, tm, ATTN_WIDTH), row),
                   pl.BlockSpec((1, tm, 256), row),
                   pl.BlockSpec((1, tm, 256), row),
                   pl.BlockSpec((1, tm, S5_WIDTH), row),
                   pl.BlockSpec((1, tm, SGU_WIDTH), row)),
        scratch_shapes=[pltpu.VMEM((d, IN_COLS), BF16)],
        compiler_params=_cparams(2),
        name="inproj_rope" if rope else "inproj_ctx",
    )(h, shift, scale, g, w_ext, gq, gk, cos_t, sin_t, ones_blk, sgun, sguw, sgub)


def _stack_group(qj, kh, lane_half):
    pieces = []
    for g in range(KV_GROUP):
        head = kh * KV_GROUP + g
        qc = qj[:, (head // 2) * LANES:(head // 2 + 1) * LANES]
        pieces.append(jnp.where(lane_half == (head % 2), qc, jnp.zeros_like(qc)))
    return jnp.concatenate(pieces, axis=0)


def _sink_col(sink_ref, kh):
    rows = lax.broadcasted_iota(jnp.int32, (KV_GROUP * WINDOW, 1), 0) // WINDOW
    col = jnp.full((KV_GROUP * WINDOW, 1), sink_ref[kh * KV_GROUP], F32)
    for g in range(1, KV_GROUP):
        col = jnp.where(rows == g, sink_ref[kh * KV_GROUP + g], col)
    return col


def _unstack_store(o_ref, o, j, kh, lane_half_f):
    lo = lane_half_f == 0
    r = slice(j * WINDOW, (j + 1) * WINDOW)
    o_ref[0, r, (2 * kh) * LANES:(2 * kh + 1) * LANES] = jnp.where(lo, o[0:128], o[128:256])
    o_ref[0, r, (2 * kh + 1) * LANES:(2 * kh + 2) * LANES] = jnp.where(lo, o[256:384], o[384:512])


def _attn_win_kernel(sink_ref, q_ref, kc_ref, vc_ref, kp_ref, vp_ref, kn_ref, vn_ref, kx_ref, vx_ref, o_ref):
    i = pl.program_id(1)
    nt = pl.num_programs(1)
    tq = q_ref.shape[1]
    nsub = tq // WINDOW
    kfull = jnp.concatenate([kp_ref[0], kc_ref[0], kn_ref[0]], axis=0)
    vfull = jnp.concatenate([vp_ref[0], vc_ref[0], vn_ref[0]], axis=0)
    rows = KV_GROUP * WINDOW
    ri = lax.broadcasted_iota(jnp.int32, (rows, 3 * WINDOW), 0) % WINDOW
    ci = lax.broadcasted_iota(jnp.int32, (rows, 3 * WINDOW), 1)
    cblk = ci // WINDOW
    cj = ci % WINDOW
    ninf = np.float32(-np.inf)
    band = (cblk == 1) | ((cblk == 0) & (cj >= ri)) | ((cblk == 2) & (cj <= ri))
    bias0 = jnp.where(band, 0.0, ninf).astype(F32)
    no_prev = jnp.where(i > 0, 0.0, ninf).astype(F32)
    no_next = jnp.where(i < nt - 1, 0.0, ninf).astype(F32)
    lane_half = lax.broadcasted_iota(jnp.int32, (WINDOW, LANES), 1) // HEAD_DIM
    for j in range(nsub):
        qj = q_ref[0, j * WINDOW:(j + 1) * WINDOW, :]
        bias = bias0
        if j == 0:
            bias = bias + jnp.where(cblk == 0, no_prev, 0.0)
        if j == nsub - 1:
            bias = bias + jnp.where(cblk == 2, no_next, 0.0)
        for kh in range(N_KV_HEADS):
            qs = _stack_group(qj, kh, lane_half)
            ks = slice(kh * LANES, (kh + 1) * LANES)
            kw = kfull[j * WINDOW:(j + 3) * WINDOW, ks]
            vw = vfull[j * WINDOW:(j + 3) * WINDOW, ks]
            s_w = _dot_nt(qs, kw) + bias
            s_c = _dot_nt(qs, kx_ref[0, :, ks])
            sink = _sink_col(sink_ref, kh)
            m = jnp.maximum(jnp.maximum(s_w.max(-1, keepdims=True), s_c.max(-1, keepdims=True)), sink)
            e_w = jnp.exp(s_w - m)
            e_c = jnp.exp(s_c - m)
            den = e_w.sum(-1, keepdims=True) + e_c.sum(-1, keepdims=True) + jnp.exp(sink - m)
            o = _dot(e_w.astype(BF16), vw) + _dot(e_c.astype(BF16), vx_ref[0, :, ks])
            _unstack_store(o_ref, o / den, j, kh, lane_half)


def _attn_ctx_kernel(sink_ref, q_ref, kx_ref, vx_ref, o_ref):
    tq = q_ref.shape[1]
    lane_half = lax.broadcasted_iota(jnp.int32, (WINDOW, LANES), 1) // HEAD_DIM
    for j in range(tq // WINDOW):
        qj = q_ref[0, j * WINDOW:(j + 1) * WINDOW, :]
        for kh in range(N_KV_HEADS):
            qs = _stack_group(qj, kh, lane_half)
            ks = slice(kh * LANES, (kh + 1) * LANES)
            s_c = _dot_nt(qs, kx_ref[0, :, ks])
            sink = _sink_col(sink_ref, kh)
            m = jnp.maximum(s_c.max(-1, keepdims=True), sink)
            e_c = jnp.exp(s_c - m)
            den = e_c.sum(-1, keepdims=True) + jnp.exp(sink - m)
            o = _dot(e_c.astype(BF16), vx_ref[0, :, ks])
            _unstack_store(o_ref, o / den, j, kh, lane_half)


def _attn_win_call(sink, q, k, v, kx, vx):
    b, n, _ = q.shape
    lc = kx.shape[1]
    tq = min(ROW_TILE, n)
    nsub = tq // WINDOW
    nblk = n // WINDOW
    assert n % tq == 0 and tq % WINDOW == 0
    row = lambda bb, i: (bb, i, 0)
    prev = lambda bb, i: (bb, jnp.maximum(i * nsub - 1, 0), 0)
    nxt = lambda bb, i: (bb, jnp.minimum((i + 1) * nsub, nblk - 1), 0)
    ctx = lambda bb, i: (bb, 0, 0)
    return pl.pallas_call(
        _attn_win_kernel,
        out_shape=jax.ShapeDtypeStruct((b, n, ATTN_WIDTH), F32),
        grid=(b, n // tq),
        in_specs=[pl.BlockSpec(memory_space=pltpu.SMEM),
                  pl.BlockSpec((1, tq, ATTN_WIDTH), row),
                  pl.BlockSpec((1, tq, 256), row), pl.BlockSpec((1, tq, 256), row),
                  pl.BlockSpec((1, WINDOW, 256), prev), pl.BlockSpec((1, WINDOW, 256), prev),
                  pl.BlockSpec((1, WINDOW, 256), nxt), pl.BlockSpec((1, WINDOW, 256), nxt),
                  pl.BlockSpec((1, lc, 256), ctx), pl.BlockSpec((1, lc, 256), ctx)],
        out_specs=pl.BlockSpec((1, tq, ATTN_WIDTH), row),
        compiler_params=_cparams(2),
        name="attn_window",
    )(sink, q, k, v, k, v, k, v, kx, vx)


def _attn_ctx_call(sink, q, kx, vx):
    b, lc, _ = q.shape
    assert lc % WINDOW == 0
    blk = lambda bb: (bb, 0, 0)
    return pl.pallas_call(
        _attn_ctx_kernel,
        out_shape=jax.ShapeDtypeStruct((b, lc, ATTN_WIDTH), F32),
        grid=(b,),
        in_specs=[pl.BlockSpec(memory_space=pltpu.SMEM),
                  pl.BlockSpec((1, lc, ATTN_WIDTH), blk),
                  pl.BlockSpec((1, lc, 256), blk), pl.BlockSpec((1, lc, 256), blk)],
        out_specs=pl.BlockSpec((1, lc, ATTN_WIDTH), blk),
        compiler_params=_cparams(1),
        name="attn_ctx",
    )(sink, q, kx, vx)


def _s5_kernel(u_ref, perm_ref, permt_ref, bre_ref, bim_ref, a_ref, al_ref, cre_ref, cim_ref, h0_ref,
               y_ref, hfin_ref, bure_ref, buim_ref, hsre_ref, hsim_ref, carry_ref):
    t = u_ref.shape[1]
    seg_len = t // S5_SEGMENTS

    @pl.when(pl.program_id(1) == 0)
    def _():
        carry_ref[...] = h0_ref[0]

    up = _dot(perm_ref[...], u_ref[0].astype(BF16)).astype(BF16)
    bure_ref[...] = _dot(up, bre_ref[...])
    buim_ref[...] = _dot(up, bim_ref[...])
    a_re = jnp.broadcast_to(a_ref[0:1, :], (S5_SEGMENTS, S5_NSTATE))
    a_im = jnp.broadcast_to(a_ref[1:2, :], (S5_SEGMENTS, S5_NSTATE))

    def step(i, h):
        hr, hi = h
        r0 = pl.multiple_of(i * S5_SEGMENTS, S5_SEGMENTS)
        nr = a_re * hr - a_im * hi + bure_ref[pl.ds(r0, S5_SEGMENTS), :]
        ni = a_re * hi + a_im * hr + buim_ref[pl.ds(r0, S5_SEGMENTS), :]
        return nr, ni

    zero = jnp.zeros((S5_SEGMENTS, S5_NSTATE), F32)
    end_re, end_im = lax.fori_loop(0, seg_len, step, (zero, zero), unroll=S5_UNROLL)

    al_re, al_im = al_ref[0:1, :], al_ref[1:2, :]
    c_re, c_im = carry_ref[0:1, :], carry_ref[1:2, :]
    in_re, in_im = [], []
    for s in range(S5_SEGMENTS):
        in_re.append(c_re)
        in_im.append(c_im)
        c_re, c_im = (al_re * c_re - al_im * c_im + end_re[s:s + 1, :],
                      al_re * c_im + al_im * c_re + end_im[s:s + 1, :])
    carry_ref[0:1, :] = c_re
    carry_ref[1:2, :] = c_im
    hfin_ref[0, 0:1, :] = c_re
    hfin_ref[0, 1:2, :] = c_im

    def step_store(i, h):
        nr, ni = step(i, h)
        r0 = pl.multiple_of(i * S5_SEGMENTS, S5_SEGMENTS)
        hsre_ref[pl.ds(r0, S5_SEGMENTS), :] = nr
        hsim_ref[pl.ds(r0, S5_SEGMENTS), :] = ni
        return nr, ni

    lax.fori_loop(0, seg_len, step_store, (jnp.concatenate(in_re, axis=0), jnp.concatenate(in_im, axis=0)),
                  unroll=S5_UNROLL)

    yp = _dot(hsre_ref[...].astype(BF16), cre_ref[...]) - _dot(hsim_ref[...].astype(BF16), cim_ref[...])
    yh, yl = _split(yp)
    y_ref[0] = _dot(permt_ref[...], yh) + _dot(permt_ref[...], yl)


def _s5_call(u, perm, permt, bre, bim, a, al, cre, cim, h0, *, reverse):
    b, n, w = u.shape
    t = perm.shape[0]
    nc = n // t
    assert n % t == 0
    if reverse:
        row = lambda bb, j: (bb, nc - 1 - j, 0)
    else:
        row = lambda bb, j: (bb, j, 0)
    fixed = lambda bb, j: (0, 0)
    return pl.pallas_call(
        _s5_kernel,
        out_shape=(jax.ShapeDtypeStruct((b, n, w), F32), jax.ShapeDtypeStruct((b, 2, S5_NSTATE), F32)),
        grid=(b, nc),
        in_specs=[pl.BlockSpec((1, t, w), row),
                  pl.BlockSpec((t, t), fixed), pl.BlockSpec((t, t), fixed),
                  pl.BlockSpec((w, S5_NSTATE), fixed), pl.BlockSpec((w, S5_NSTATE), fixed),
                  pl.BlockSpec((2, S5_NSTATE), fixed), pl.BlockSpec((2, S5_NSTATE), fixed),
                  pl.BlockSpec((S5_NSTATE, w), fixed), pl.BlockSpec((S5_NSTATE, w), fixed),
                  pl.BlockSpec((1, 2, S5_NSTATE), lambda bb, j: (bb, 0, 0))],
        out_specs=(pl.BlockSpec((1, t, w), row),
                   pl.BlockSpec((1, 2, S5_NSTATE), lambda bb, j: (bb, 0, 0))),
        scratch_shapes=[pltpu.VMEM((t, S5_NSTATE), F32)] * 4 + [pltpu.VMEM((2, S5_NSTATE), F32)],
        compiler_params=_cparams(2),
        name="s5_scan_rev" if reverse else "s5_scan_fwd",
    )(u, perm, permt, bre, bim, a, al, cre, cim, h0)


def _s5_perm(t, reverse):
    seg_len = t // S5_SEGMENTS
    r = np.arange(t)
    src = (r % S5_SEGMENTS) * seg_len + r // S5_SEGMENTS
    if reverse:
        src = t - 1 - src
    p = np.zeros((t, t), np.float32)
    p[r, src] = 1.0
    return jnp.asarray(p, BF16), jnp.asarray(p.T, BF16)


def _s5_params(lam_re, lam_im, log_dt, b_re, b_im, c_re, c_im, seg_lens):
    lead = lam_re.shape[:-2]
    dt = jnp.exp(log_dt.astype(F32))[..., None]
    lr, li = lam_re.astype(F32), lam_im.astype(F32)
    mag = jnp.exp(lr * dt)
    ang = li * dt
    ab_re, ab_im = mag * jnp.cos(ang), mag * jnp.sin(ang)
    nr, ni = ab_re - 1.0, ab_im
    den = lr * lr + li * li
    coef_re = (nr * lr + ni * li) / den
    coef_im = (ni * lr - nr * li) / den
    br, bi = b_re.astype(F32), b_im.astype(F32)
    bb_re = coef_re[..., None] * br - coef_im[..., None] * bi
    bb_im = coef_re[..., None] * bi + coef_im[..., None] * br
    eye = jnp.eye(S5_GROUPS, dtype=F32)

    def in_mat(bb):
        return jnp.einsum("...gpc,gh->...gchp", bb, eye).reshape(*lead, S5_WIDTH, S5_NSTATE).astype(BF16)

    def out_mat(cc):
        return jnp.einsum("...gcp,gh->...gphc", cc.astype(F32), eye).reshape(*lead, S5_NSTATE, S5_WIDTH).astype(BF16)

    flat = lambda v: v.reshape(*lead, S5_NSTATE)
    a = jnp.stack([flat(ab_re), flat(ab_im)], axis=-2)
    powers, pr, pi, length = {}, ab_re, ab_im, 1
    while length < max(seg_lens):
        pr, pi = pr * pr - pi * pi, 2.0 * pr * pi
        length *= 2
        if length in seg_lens:
            powers[length] = jnp.stack([flat(pr), flat(pi)], axis=-2)
    assert set(powers) == set(seg_lens)
    return in_mat(bb_re), in_mat(bb_im), a, powers, out_mat(c_re), out_mat(c_im)


def _mix_kernel(h_ref, att_ref, yf_ref, yb_ref, us_ref, sg_ref, gate_ref, shift_ref, scale_ref,
                dskip_ref, gluw_ref, glub_ref, outg_ref, wout_ref, gffn_ref, rw_ref,
                hnew_ref, f_ref, p_ref, w16_ref):
    @pl.when((pl.program_id(0) == 0) & (pl.program_id(1) == 0))
    def _():
        w16_ref[...] = wout_ref[0].astype(BF16)

    y = yf_ref[0] + yb_ref[0] + dskip_ref[...] * us_ref[0]
    g = _gelu_tanh(y)
    s5 = g * _sigmoid(_dot(g.astype(BF16), gluw_ref[...]) + glub_ref[...])
    outg = outg_ref[...]
    o1, o2 = ATTN_WIDTH, ATTN_WIDTH + S5_WIDTH
    pa = _rms(att_ref[0], outg[:, :o1]).astype(BF16)
    ps = _rms(s5, outg[:, o1:o2]).astype(BF16)
    pg = _rms(sg_ref[0], outg[:, o2:]).astype(BF16)
    mixed = _dot(pa, w16_ref[:o1, :]) + _dot(ps, w16_ref[o1:o2, :]) + _dot(pg, w16_ref[o2:, :])
    hn = h_ref[0] + gate_ref[0] * mixed
    hnew_ref[0] = hn

    f = _rms(hn, gffn_ref[...]) * (1.0 + scale_ref[0]) + shift_ref[0]
    fh, fl = _split(f)
    f_ref[0] = fh
    rh, rl = _split(rw_ref[...])
    logits = _dot_nt(rh, fh) + _dot_nt(rh, fl) + _dot_nt(rl, fh)
    m = logits.max(axis=0, keepdims=True)
    e = jnp.exp(logits - m)
    p_ref[0] = e / e.sum(axis=0, keepdims=True)


def _mix_call(layer, h, att, yf, yb, us, sg, gate, shift, scale, dskip, gluw, glub, outg, wout, gffn, rwt):
    b, n, d = h.shape
    tm = min(ROW_TILE, n)
    assert n % tm == 0
    row = lambda bb, i: (bb, i, 0)
    vec = lambda bb, i: (bb, 0, 0)
    fixed = lambda bb, i: (0, 0)
    return pl.pallas_call(
        _mix_kernel,
        out_shape=(jax.ShapeDtypeStruct((b, n, d), F32),
                   jax.ShapeDtypeStruct((b, n, d), BF16),
                   jax.ShapeDtypeStruct((b, N_EXPERTS, n), F32)),
        grid=(b, n // tm),
        in_specs=[pl.BlockSpec((1, tm, d), row),
                  pl.BlockSpec((1, tm, ATTN_WIDTH), row),
                  pl.BlockSpec((1, tm, S5_WIDTH), row), pl.BlockSpec((1, tm, S5_WIDTH), row),
                  pl.BlockSpec((1, tm, S5_WIDTH), row), pl.BlockSpec((1, tm, SGU_WIDTH), row),
                  pl.BlockSpec((1, 1, d), vec), pl.BlockSpec((1, 1, d), vec), pl.BlockSpec((1, 1, d), vec),
                  pl.BlockSpec((1, S5_WIDTH), fixed),
                  pl.BlockSpec((S5_WIDTH, S5_WIDTH), fixed),
                  pl.BlockSpec((1, S5_WIDTH), fixed),
                  pl.BlockSpec((1, d), fixed),
                  pl.BlockSpec((1, d, d), lambda bb, i: (layer, 0, 0)),
                  pl.BlockSpec((1, d), fixed),
                  pl.BlockSpec((N_EXPERTS, d), fixed)],
        out_specs=(pl.BlockSpec((1, tm, d), row), pl.BlockSpec((1, tm, d), row),
                   pl.BlockSpec((1, N_EXPERTS, tm), lambda bb, i: (bb, 0, i))),
        scratch_shapes=[pltpu.VMEM((d, d), BF16)],
        compiler_params=_cparams(2),
        name="mix_out_router",
    )(h, att, yf, yb, us, sg, gate, shift, scale, dskip, gluw, glub, outg, wout, gffn, rwt)


def _thr_kernel(p_ref, thr_ref, allow_ref, *, cap):
    p = p_ref[0]

    def count(mask):
        return jnp.sum(mask.astype(F32), axis=1, keepdims=True)

    def as_float(bits):
        return pltpu.bitcast(jnp.broadcast_to(bits, (N_EXPERTS, LANES)), F32)[:, 0:1]

    def body(it, thr_bits):
        cand = thr_bits | jnp.left_shift(jnp.int32(1), 30 - it)
        return jnp.where(count(p >= as_float(cand)) >= cap, cand, thr_bits)

    thr = as_float(lax.fori_loop(0, 31, body, jnp.zeros((N_EXPERTS, 1), jnp.int32)))
    allow = (cap - count(p > thr)).astype(jnp.int32)
    thr_ref[0] = jnp.broadcast_to(thr, (N_EXPERTS, LANES))
    allow_ref[0] = jnp.broadcast_to(allow, (N_EXPERTS, LANES))


def _thr_call(probs, cap):
    b, e, n = probs.shape
    blk = lambda bb: (bb, 0, 0)
    return pl.pallas_call(
        functools.partial(_thr_kernel, cap=cap),
        out_shape=(jax.ShapeDtypeStruct((b, e, LANES), F32), jax.ShapeDtypeStruct((b, e, LANES), jnp.int32)),
        grid=(b,),
        in_specs=[pl.BlockSpec((1, e, n), blk)],
        out_specs=(pl.BlockSpec((1, e, LANES), blk),) * 2,
        compiler_params=_cparams(1),
        name="route_threshold",
    )(probs)


def _rank_kernel(p_ref, thr_ref, allow_ref, rank_ref, offs_ref, eqseen_ref, selseen_ref):
    @pl.when(pl.program_id(1) == 0)
    def _():
        eqseen_ref[...] = jnp.zeros_like(eqseen_ref)
        selseen_ref[...] = jnp.zeros_like(selseen_ref)

    tt = p_ref.shape[2]
    p = p_ref[0]
    thr = thr_ref[0][:, 0:1]
    allow = allow_ref[0][:, 0:1].astype(F32)
    gt = p > thr
    eq = p == thr
    before = (lax.broadcasted_iota(jnp.int32, (tt, tt), 0) <
              lax.broadcasted_iota(jnp.int32, (tt, tt), 1)).astype(F32).astype(BF16)
    eq_seen = eqseen_ref[...][:, 0:1]
    eqf = eq.astype(F32)
    eq_before = eq_seen + _dot(eqf.astype(BF16), before)
    sel = gt | (eq & (eq_before < allow))
    self32 = sel.astype(F32)
    sel_seen = selseen_ref[...][:, 0:1]
    rank = sel_seen + _dot(self32.astype(BF16), before)
    rank_ref[0] = jnp.where(sel, rank, -1.0)
    offs_ref[0, 0] = jnp.broadcast_to(sel_seen, (N_EXPERTS, LANES)).astype(jnp.int32)
    eqseen_ref[...] = jnp.broadcast_to(eq_seen + eqf.sum(axis=1, keepdims=True), (N_EXPERTS, LANES))
    selseen_ref[...] = jnp.broadcast_to(sel_seen + self32.sum(axis=1, keepdims=True), (N_EXPERTS, LANES))


def _rank_call(probs, thr, allow, tt):
    b, e, n = probs.shape
    nt = n // tt
    blk = lambda bb, t: (bb, 0, 0)
    return pl.pallas_call(
        _rank_kernel,
        out_shape=(jax.ShapeDtypeStruct((b, e, n), F32),
                   jax.ShapeDtypeStruct((b, nt, e, LANES), jnp.int32)),
        grid=(b, nt),
        in_specs=[pl.BlockSpec((1, e, tt), lambda bb, t: (bb, 0, t)),
                  pl.BlockSpec((1, e, LANES), blk), pl.BlockSpec((1, e, LANES), blk)],
        out_specs=(pl.BlockSpec((1, e, tt), lambda bb, t: (bb, 0, t)),
                   pl.BlockSpec((1, 1, e, LANES), lambda bb, t: (bb, t, 0, 0))),
        scratch_shapes=[pltpu.VMEM((e, LANES), F32)] * 2,
        compiler_params=_cparams(2),
        name="route_rank",
    )(probs, thr, allow)


def _window_rounds(offs_sm, row, base, win):
    o0, o1 = offs_sm[row], offs_sm[row + 1]
    return jnp.where(o1 > o0, (o1 - base + win - 1) // win, 0)


def _window_hit(rank_row, lo, start, iota):
    rel = jnp.where(rank_row >= lo.astype(F32), rank_row, -1.0) - start.astype(F32)
    return rel == iota


def _gather_kernel(offs_sm, rank_ref, p_ref, x_ref, xin_ref, gsel_ref, *, win, tt):
    bb, eg, t = pl.program_id(0), pl.program_id(1), pl.program_id(2)
    grp, cap = xin_ref.shape[1], xin_ref.shape[2]
    sub = x_ref.shape[1] // tt
    n_tiles = pl.num_programs(2) * sub

    @pl.when(t == 0)
    def _():
        xin_ref[...] = jnp.zeros_like(xin_ref)
        gsel_ref[...] = jnp.zeros_like(gsel_ref)

    iota = lax.broadcasted_iota(jnp.int32, (win, tt), 0).astype(F32)
    for s in range(sub):
        cols = slice(s * tt, (s + 1) * tt)
        base, rounds = [], jnp.int32(0)
        for g in range(grp):
            row = ((bb * pl.num_programs(1) + eg) * grp + g) * (n_tiles + 1) + t * sub + s
            base.append((offs_sm[row] // ROW_ALIGN) * ROW_ALIGN)
            rounds = jnp.maximum(rounds, _window_rounds(offs_sm, row, base[g], win))

        def one_round(r, carry, cols=cols, base=base):
            hits, starts = [], []
            for g in range(grp):
                lo = base[g] + r * win
                starts.append(pl.multiple_of(jnp.minimum(lo, cap - win), ROW_ALIGN))
                hits.append(_window_hit(rank_ref[0, pl.ds(eg * grp + g, 1), cols], lo, starts[g], iota))
            onehot = jnp.concatenate([h.astype(F32).astype(BF16) for h in hits], axis=0)
            res = _dot(onehot, x_ref[0, cols, :])
            for g in range(grp):
                rows = pl.ds(starts[g], win)
                xin_ref[0, g, rows, :] += res[g * win:(g + 1) * win].astype(BF16)
                gate = jnp.sum(jnp.where(hits[g], p_ref[0, pl.ds(eg * grp + g, 1), cols], 0.0),
                               axis=1, keepdims=True)
                gsel_ref[0, g, rows, :] += jnp.broadcast_to(gate, (win, LANES))
            return carry

        lax.fori_loop(0, rounds, one_round, 0)


def _gather_call(offs1, rank, probs, f16, cap, win, tt):
    b, n, d = f16.shape
    e, grp = N_EXPERTS, MOE_GROUP
    sub = min(GATHER_TILES, n // tt)
    assert n % (sub * tt) == 0
    grid_spec = pltpu.PrefetchScalarGridSpec(
        num_scalar_prefetch=1,
        grid=(b, e // grp, n // (sub * tt)),
        in_specs=[pl.BlockSpec((1, e, sub * tt), lambda bb, gg, t, o: (bb, 0, t)),
                  pl.BlockSpec((1, e, sub * tt), lambda bb, gg, t, o: (bb, 0, t)),
                  pl.BlockSpec((1, sub * tt, d), lambda bb, gg, t, o: (bb, t, 0))],
        out_specs=(pl.BlockSpec((1, grp, cap, d), lambda bb, gg, t, o: (bb, gg, 0, 0)),
                   pl.BlockSpec((1, grp, cap, LANES), lambda bb, gg, t, o: (bb, gg, 0, 0))),
    )
    return pl.pallas_call(
        functools.partial(_gather_kernel, win=win, tt=tt),
        out_shape=(jax.ShapeDtypeStruct((b, e, cap, d), BF16),
                   jax.ShapeDtypeStruct((b, e, cap, LANES), F32)),
        grid_spec=grid_spec,
        compiler_params=_cparams(3),
        name="moe_gather",
    )(offs1, rank, probs, f16)


def _ffn_kernel(*refs, has_ctx, layer):
    if has_ctx:
        x_ref, g_ref, xc_ref, gc_ref, wg_hbm, wu_hbm, wd_hbm, y_ref, yc_ref, w32, w16, sem = refs
    else:
        x_ref, g_ref, wg_hbm, wu_hbm, wd_hbm, y_ref, w32, w16, sem = refs
    e, m = pl.program_id(0), pl.program_id(2)
    n_e = pl.num_programs(0)

    def weight_copies(expert):
        return [pltpu.make_async_copy(w_hbm.at[layer, expert], w32.at[k], sem.at[k])
                for k, w_hbm in enumerate((wg_hbm, wu_hbm, wd_hbm))]

    @pl.when((pl.program_id(1) == 0) & (m == 0))
    def _():
        @pl.when(e == 0)
        def _():
            for cp in weight_copies(e):
                cp.start()

        for cp in weight_copies(e):
            cp.wait()
        for k in range(3):
            w16[k] = w32[k].astype(BF16)

        @pl.when(e + 1 < n_e)
        def _():
            for cp in weight_copies(e + 1):
                cp.start()

    def ffn(x, gate):
        hg = _dot(x, w16[0])
        hid = (hg * _sigmoid(hg)) * _dot(x, w16[1])
        return (_dot(hid.astype(BF16), w16[2]) * gate).astype(BF16)

    if not has_ctx:
        y_ref[0, 0] = ffn(x_ref[0, 0], g_ref[0, 0][:, 0:1])
        return
    tm = x_ref.shape[2]

    @pl.when(m == 0)
    def _():
        x = jnp.concatenate([x_ref[0, 0], xc_ref[0, 0]], axis=0)
        gate = jnp.concatenate([g_ref[0, 0][:, 0:1], gc_ref[0, 0][:, 0:1]], axis=0)
        y = ffn(x, gate)
        y_ref[0, 0] = y[:tm]
        yc_ref[0, 0] = y[tm:]

    @pl.when(m > 0)
    def _():
        y_ref[0, 0] = ffn(x_ref[0, 0], g_ref[0, 0][:, 0:1])


def _ffn_call(layer, xin, gsel, wg, wu, wd, ctx_rows=None):
    b, e, cap, d = xin.shape
    ff = wg.shape[3]
    assert wg.shape[2:] == (d, ff) and wu.shape[2:] == (d, ff) and wd.shape[2:] == (ff, d) and ff == d
    tm = min(FFN_ROW_TILE, cap)
    assert cap % tm == 0
    xrow = lambda ee, bb, m: (bb, ee, m, 0)
    xctx = lambda ee, bb, m: (bb, ee, 0, 0)
    in_specs = [pl.BlockSpec((1, 1, tm, d), xrow), pl.BlockSpec((1, 1, tm, LANES), xrow)]
    out_shape = [jax.ShapeDtypeStruct((b, e, cap, d), BF16)]
    out_specs = [pl.BlockSpec((1, 1, tm, d), xrow)]
    args = [xin, gsel]
    if ctx_rows is not None:
        xc, gc = ctx_rows
        capc = xc.shape[2]
        in_specs += [pl.BlockSpec((1, 1, capc, d), xctx), pl.BlockSpec((1, 1, capc, LANES), xctx)]
        out_shape.append(jax.ShapeDtypeStruct((b, e, capc, d), BF16))
        out_specs.append(pl.BlockSpec((1, 1, capc, d), xctx))
        args += [xc, gc]
    in_specs += [pl.BlockSpec(memory_space=pl.ANY)] * 3
    return pl.pallas_call(
        functools.partial(_ffn_kernel, has_ctx=ctx_rows is not None, layer=layer),
        out_shape=tuple(out_shape),
        grid=(e, b, cap // tm),
        in_specs=in_specs,
        out_specs=tuple(out_specs),
        scratch_shapes=[pltpu.VMEM((3, d, ff), F32), pltpu.VMEM((3, d, ff), BF16), pltpu.SemaphoreType.DMA((3,))],
        compiler_params=_cparams(3),
        name="moe_expert_ffn",
    )(*args, wg, wu, wd)


def _combine_kernel(offs_sm, rank_ref, y_hbm, h_ref, gate_ref, o_ref, win_ref, sem, *, win):
    bb, t = pl.program_id(0), pl.program_id(1)
    n_b, n_t = pl.num_programs(0), pl.num_programs(1)
    step = bb * n_t + t
    slot = step % 2
    n_e = rank_ref.shape[1]
    cap = y_hbm.shape[2]
    tt = h_ref.shape[1]

    def offs_row(b_i, e, t_i):
        return (b_i * n_e + e) * (n_t + 1) + t_i

    def bases(b_i, t_i):
        return [(offs_sm[offs_row(b_i, e, t_i)] // ROW_ALIGN) * ROW_ALIGN for e in range(n_e)]

    def window_start(lo):
        return pl.multiple_of(jnp.minimum(lo, cap - win), ROW_ALIGN)

    def copies(b_i, base, r, slot_i):
        return [pltpu.make_async_copy(y_hbm.at[b_i, e, pl.ds(window_start(base[e] + r * win), win), :],
                                      win_ref.at[slot_i, pl.ds(e * win, win), :],
                                      sem.at[slot_i, e])
                for e in range(n_e)]

    @pl.when(step == 0)
    def _():
        for cp in copies(bb, bases(bb, t), 0, slot):
            cp.start()

    @pl.when(step + 1 < n_b * n_t)
    def _():
        wrap = t + 1 == n_t
        b2 = jnp.where(wrap, bb + 1, bb)
        t2 = jnp.where(wrap, 0, t + 1)
        for cp in copies(b2, bases(b2, t2), 0, 1 - slot):
            cp.start()

    base = bases(bb, t)
    rounds = jnp.int32(0)
    for e in range(n_e):
        rounds = jnp.maximum(rounds, _window_rounds(offs_sm, offs_row(bb, e, t), base[e], win))
    iota = lax.broadcasted_iota(jnp.int32, (win, tt), 0).astype(F32)

    def contribution(r):
        hits = []
        for e in range(n_e):
            lo = base[e] + r * win
            hit = _window_hit(rank_ref[0, e:e + 1, :], lo, window_start(lo), iota)
            hits.append(hit.astype(F32).astype(BF16))
        return _dot_tn(jnp.concatenate(hits, axis=0), win_ref[slot])

    for cp in copies(bb, base, 0, slot):
        cp.wait()
    o_ref[0] = h_ref[0] + gate_ref[0] * contribution(0)

    def extra_round(r, carry):
        cps = copies(bb, base, r, slot)
        for cp in cps:
            cp.start()
        for cp in cps:
            cp.wait()
        o_ref[0] += gate_ref[0] * contribution(r)
        return carry

    lax.fori_loop(1, rounds, extra_round, 0)


def _combine_call(offs1, rank, y, h, gate, win, tt):
    b, n, d = h.shape
    e = N_EXPERTS
    grid_spec = pltpu.PrefetchScalarGridSpec(
        num_scalar_prefetch=1,
        grid=(b, n // tt),
        in_specs=[pl.BlockSpec((1, e, tt), lambda bb, t, o: (bb, 0, t)),
                  pl.BlockSpec(memory_space=pl.ANY),
                  pl.BlockSpec((1, tt, d), lambda bb, t, o: (bb, t, 0)),
                  pl.BlockSpec((1, 1, d), lambda bb, t, o: (bb, 0, 0))],
        out_specs=pl.BlockSpec((1, tt, d), lambda bb, t, o: (bb, t, 0)),
        scratch_shapes=[pltpu.VMEM((2, e * win, d), BF16), pltpu.SemaphoreType.DMA((2, e))],
    )
    return pl.pallas_call(
        functools.partial(_combine_kernel, win=win),
        out_shape=jax.ShapeDtypeStruct((b, n, d), F32),
        grid_spec=grid_spec,
        compiler_params=_cparams(2),
        name="moe_combine",
    )(offs1, rank, y, h, gate)


def _route(probs):
    b, e, n = probs.shape
    cap = EC_CAPACITY_FACTOR * n // e
    tt = min(ROUTE_TILE, n)
    thr, allow = _thr_call(probs, cap)
    rank, offs = _rank_call(probs, thr, allow, tt)
    offs = jnp.transpose(offs[..., 0], (0, 2, 1))
    offs1 = jnp.concatenate([offs, jnp.full((b, e, 1), cap, jnp.int32)], axis=-1)
    return rank, offs1.reshape(-1), cap, min(ROUTE_BLOCK, cap), tt


def _rope_tables(n):
    rows = n // GRID_W
    row = jnp.repeat(jnp.arange(rows), GRID_W).astype(F32)
    col = jnp.tile(jnp.arange(GRID_W), rows).astype(F32)
    axis_dim = HEAD_DIM // 2
    inv = ROPE_BASE ** (-jnp.arange(0, axis_dim, 2, dtype=F32) / axis_dim)
    ang_r = row[:, None] * inv
    ang_c = col[:, None] * inv
    cr, sr, cc, sc = jnp.cos(ang_r), jnp.sin(ang_r), jnp.cos(ang_c), jnp.sin(ang_c)
    cos64 = jnp.concatenate([cr, cr, cc, cc], axis=1)
    sin64 = jnp.concatenate([-sr, sr, -sc, sc], axis=1)
    return jnp.tile(cos64, (1, 2)), jnp.tile(sin64, (1, 2))


def _extend_w_in(w_in):
    q = w_in[:, :512]
    k = w_in[:, 512:640]
    v = w_in[:, 640:768]
    rest = w_in[:, 768:]
    dup = lambda m: jnp.concatenate([m[:, :64], m[:, :64], m[:, 64:], m[:, 64:]], axis=1)
    return jnp.concatenate([q, dup(k), dup(v), rest], axis=1)


def kernel(x, c, ctx, c_ctx, ada_w, ada_b, norm_mix_g, norm_ffn_g, w_in, q_norm_g, k_norm_g, attn_sink,
           s5_lambda_re, s5_lambda_im, s5_log_dt, s5_b_re, s5_b_im, s5_c_re, s5_c_im, s5_d, s5_glu_w,
           s5_glu_b, sgu_norm_g, sgu_w, sgu_b, out_norm_g, w_out, router_w, exp_w_gate, exp_w_up,
           exp_w_down):
    b, n, d = x.shape
    lc = ctx.shape[1]
    depth = ada_w.shape[0]
    assert b + 1 <= SUBLANES

    crows = jnp.zeros((SUBLANES, d), F32).at[:b].set(c).at[b].set(c_ctx)
    mod = _ada_call(crows, ada_w, ada_b)

    cos_t, sin_t = _rope_tables(n)
    cos_c, sin_c = cos_t[:lc], sin_t[:lc]
    ones_blk = jnp.asarray(np.arange(ATTN_WIDTH)[:, None] // HEAD_DIM == np.arange(ATTN_WIDTH)[None, :] // HEAD_DIM,
                           BF16)
    t_lat = min(S5_CHUNK, n)
    t_ctx = min(S5_CHUNK, lc)
    perms = {(t, r): _s5_perm(t, r) for t in {t_lat, t_ctx} for r in (False, True)}
    seg_lat, seg_ctx = t_lat // S5_SEGMENTS, t_ctx // S5_SEGMENTS
    s5_bre, s5_bim, s5_a, s5_apow, s5_cre, s5_cim = _s5_params(
        s5_lambda_re, s5_lambda_im, s5_log_dt, s5_b_re, s5_b_im, s5_c_re, s5_c_im, {seg_lat, seg_ctx})

    h_lat, h_ctx = x, ctx
    for l in range(depth):
        ctx_out = l < depth - 1
        mod_lat = mod[l, :b].reshape(b, N_MOD, 1, d)
        mod_ctx = jnp.broadcast_to(mod[l, b].reshape(1, N_MOD, 1, d), (b, N_MOD, 1, d))
        row = lambda v: v.reshape(1, -1)

        w_ext = _extend_w_in(w_in[l])
        gq = row(jnp.tile(q_norm_g[l], N_HEADS) * np.float32(HEAD_DIM ** -0.5))
        gk = row(jnp.tile(k_norm_g[l], 2 * N_KV_HEADS))
        sgub = jnp.repeat(sgu_b[l].T, SGU_WIDTH // SGU_GROUPS, axis=1)
        common = (row(norm_mix_g[l]), w_ext, gq, gk)
        sgu_args = (ones_blk, row(sgu_norm_g[l]), sgu_w[l], sgub)
        q_l, k_l, v_l, s_l, sg_l = _inproj_call(h_lat, mod_lat[:, 0], mod_lat[:, 1], *common, cos_t, sin_t,
                                                *sgu_args, rope=True)
        q_c, k_c, v_c, s_c, sg_c = _inproj_call(h_ctx, mod_ctx[:, 0], mod_ctx[:, 1], *common, cos_c, sin_c,
                                                *sgu_args, rope=False)

        att_lat = _attn_win_call(attn_sink[l], q_l, k_l, v_l, k_c, v_c)

        ys_lat, ys_ctx = [], []
        for direction in range(2):
            rev = direction == 1
            ld = (l, direction)
            bre, bim, a, cre, cim = s5_bre[ld], s5_bim[ld], s5_a[ld], s5_cre[ld], s5_cim[ld]
            h0 = jnp.zeros((b, 2, S5_NSTATE), F32)
            y_c, hfin = _s5_call(s_c, *perms[(t_ctx, rev)], bre, bim, a, s5_apow[seg_ctx][ld], cre, cim, h0,
                                 reverse=rev)
            y_l, _ = _s5_call(s_l, *perms[(t_lat, rev)], bre, bim, a, s5_apow[seg_lat][ld], cre, cim, hfin,
                              reverse=rev)
            ys_lat.append(y_l)
            ys_ctx.append(y_c)

        tail = (row(s5_d[l]), s5_glu_w[l].astype(BF16), row(s5_glu_b[l]), row(out_norm_g[l]), w_out,
                row(norm_ffn_g[l]), router_w[l].T)
        experts = (exp_w_gate, exp_w_up, exp_w_down)
        h_mid, f_lat, p_lat = _mix_call(l, h_lat, att_lat, ys_lat[0], ys_lat[1], s_l, sg_l,
                                        mod_lat[:, 2], mod_lat[:, 3], mod_lat[:, 4], *tail)
        rank_l, offs_l, cap_l, win_l, tt_l = _route(p_lat)
        xin_l, gsel_l = _gather_call(offs_l, rank_l, p_lat, f_lat, cap_l, win_l, tt_l)
        if ctx_out:
            att_ctx = _attn_ctx_call(attn_sink[l], q_c, k_c, v_c)
            hc_mid, f_ctx, p_ctx = _mix_call(l, h_ctx, att_ctx, ys_ctx[0], ys_ctx[1], s_c, sg_c,
                                             mod_ctx[:, 2], mod_ctx[:, 3], mod_ctx[:, 4], *tail)
            rank_c, offs_c, cap_c, win_c, tt_c = _route(p_ctx)
            xin_c, gsel_c = _gather_call(offs_c, rank_c, p_ctx, f_ctx, cap_c, win_c, tt_c)
            y_l, y_c = _ffn_call(l, xin_l, gsel_l, *experts, ctx_rows=(xin_c, gsel_c))
            h_ctx = _combine_call(offs_c, rank_c, y_c, hc_mid, mod_ctx[:, 5], win_c, tt_c)
        else:
            (y_l,) = _ffn_call(l, xin_l, gsel_l, *experts)
        h_lat = _combine_call(offs_l, rank_l, y_l, h_mid, mod_lat[:, 5], win_l, tt_l)
    return h_lat
```
